```python
import jax, jax.numpy as jnp
from jax import lax
import numpy as np

D_MODEL = 1024
BATCH = 32
SEQ = 2048
DEPTH = 2

CHUNK = 64
HEAD_DIM = 64
POOL_WIDTH = D_MODEL // 4
POOL_WINDOWS = (2, 4, 8, 16)
POOL_GROUPS = len(POOL_WINDOWS)
POOL_GROUP_DIM = POOL_WIDTH // POOL_GROUPS
SGU_WIDTH = D_MODEL // 4
SGU_HEADS = SGU_WIDTH // HEAD_DIM
SGU_BLOCK = 2 * CHUNK
SB_WIDTH = D_MODEL - POOL_WIDTH - SGU_WIDTH
SB_HEADS = SB_WIDTH // HEAD_DIM
ATTN_BLOCK = 2 * CHUNK
IN_WIDTHS = (POOL_WIDTH, POOL_WIDTH,
             SGU_WIDTH, SGU_WIDTH, SGU_WIDTH,
             SB_WIDTH, SB_WIDTH, SB_WIDTH, SB_WIDTH)
IN_WIDTH = sum(IN_WIDTHS)
DN_ALPHA = (2 * DEPTH) ** 0.25
DN_BETA = (8 * DEPTH) ** -0.25
LN_EPS = 1e-5

kernel_name = "hybrid_pool_sgu_stickbreak_deepnorm_adaln"


def _layer_norm(x, g, b):
    xf = x.astype(jnp.float32)
    mu = jnp.mean(xf, axis=-1, keepdims=True)
    var = jnp.mean(jnp.square(xf - mu), axis=-1, keepdims=True)
    y = (xf - mu) * lax.rsqrt(var + LN_EPS)
    return (y * g.astype(jnp.float32) + b.astype(jnp.float32)).astype(x.dtype)


def _pool_mixer(a, w, scale):
    B, S, _ = a.shape
    af = a.astype(jnp.float32)
    cs = jnp.cumsum(af, axis=1)
    pos = jnp.arange(S)
    means = []
    for g, win in enumerate(POOL_WINDOWS):
        csg = cs[..., g * POOL_GROUP_DIM:(g + 1) * POOL_GROUP_DIM]
        lag = jnp.pad(csg, ((0, 0), (win, 0), (0, 0)))[:, :S]
        cnt = jnp.minimum(pos + 1, win).astype(jnp.float32)[None, :, None]
        means.append((csg - lag) / cnt)
    mean = jnp.stack(means, axis=2)
    d = (mean - af.reshape(B, S, POOL_GROUPS, POOL_GROUP_DIM)).astype(a.dtype)
    y = jnp.einsum('bsgc,gcd->bsgd', d, w).reshape(B, S, POOL_WIDTH)
    return y * scale


def _spatial_gating(u, v, ln_g, ln_b, w_s, b_s):
    B, S, _ = u.shape
    v = _layer_norm(v, ln_g, ln_b)
    t = jnp.arange(SGU_BLOCK)
    mask = (t[None, :] // CHUNK) <= (t[:, None] // CHUNK)
    w = jnp.where(mask[None], w_s, 0.0)
    vb = v.reshape(B, S // SGU_BLOCK, SGU_BLOCK, SGU_HEADS, HEAD_DIM)
    mixed = jnp.einsum('hts,bnshd->bnthd', w, vb) + b_s.T[None, None, :, :, None]
    return u * mixed.reshape(B, S, SGU_WIDTH)


def _stick_breaking(q, k, v):
    B, S, _ = q.shape
    q = q.reshape(B, S, SB_HEADS, HEAD_DIM).transpose(0, 2, 1, 3)
    k = k.reshape(B, S, SB_HEADS, HEAD_DIM).transpose(0, 2, 1, 3)
    v = v.reshape(B, S, SB_HEADS, HEAD_DIM).transpose(0, 2, 1, 3)
    inv_sqrt_d = HEAD_DIM ** -0.5
    outs = []
    for i in range(S // ATTN_BLOCK):
        start = i * ATTN_BLOCK
        end = start + ATTN_BLOCK
        qb = q[:, :, start:end]
        kb = k[:, :, :end]
        vb = v[:, :, :end]
        z = jnp.einsum('bhtd,bhsd->bhts', qb, kb).astype(jnp.float32) * inv_sqrt_d
        tpos = start + jnp.arange(ATTN_BLOCK)
        spos = jnp.arange(end)
        mask = spos[None, :] < tpos[:, None]
        log_beta = jax.nn.log_sigmoid(z)
        log_1m_beta = jnp.where(mask, log_beta - z, 0.0)
        later = lax.cumsum(log_1m_beta, axis=3, reverse=True) - log_1m_beta
        a = jnp.where(mask, jnp.exp(log_beta + later), 0.0)
        outs.append(jnp.einsum('bhts,bhsd->bhtd', a.astype(v.dtype), vb))
    o = jnp.concatenate(outs, axis=2)
    return o.transpose(0, 2, 1, 3).reshape(B, S, SB_WIDTH)


def setup_inputs(seed: int = 0) -> dict:
    key = jax.random.key(seed)
    ks = jax.random.split(key, 16)
    f32 = jnp.float32
    nrm = lambda k, shape, s: jax.random.normal(k, shape, f32) * s
    x = jax.random.normal(ks[0], (BATCH, SEQ, D_MODEL), f32)
    c = jax.random.normal(ks[1], (BATCH, D_MODEL), f32)
    w_in = nrm(ks[2], (DEPTH, D_MODEL, IN_WIDTH), D_MODEL ** -0.5)
    pool_w = nrm(ks[3], (DEPTH, POOL_GROUPS, POOL_GROUP_DIM, POOL_GROUP_DIM), POOL_GROUP_DIM ** -0.5)
    pool_scale = 1.0 + nrm(ks[4], (DEPTH, POOL_WIDTH), 0.05)
    sgu_ln_g = 1.0 + nrm(ks[5], (DEPTH, SGU_WIDTH), 0.02)
    sgu_ln_b = nrm(ks[6], (DEPTH, SGU_WIDTH), 0.02)
    sgu_w = nrm(ks[7], (DEPTH, SGU_HEADS, SGU_BLOCK, SGU_BLOCK), SGU_BLOCK ** -0.5)
    sgu_b = 1.0 + nrm(ks[8], (DEPTH, SGU_HEADS, SGU_BLOCK), 0.02)
    w_out = nrm(ks[9], (DEPTH, D_MODEL, D_MODEL), DN_BETA * D_MODEL ** -0.5)
    ada_w = nrm(ks[10], (DEPTH, D_MODEL, 3 * D_MODEL), D_MODEL ** -0.5)
    ada_b = nrm(ks[11], (DEPTH, 3 * D_MODEL), 0.02)
    ln_g = 1.0 + nrm(ks[12], (DEPTH, D_MODEL), 0.02)
    ln_b = nrm(ks[13], (DEPTH, D_MODEL), 0.02)
    return {"x": x, "c": c, "w_in": w_in, "pool_w": pool_w, "pool_scale": pool_scale,
            "sgu_ln_g": sgu_ln_g, "sgu_ln_b": sgu_ln_b, "sgu_w": sgu_w, "sgu_b": sgu_b,
            "w_out": w_out, "ada_w": ada_w, "ada_b": ada_b, "ln_g": ln_g, "ln_b": ln_b}


def reference(x, c, w_in, pool_w, pool_scale, sgu_ln_g, sgu_ln_b, sgu_w, sgu_b,
              w_out, ada_w, ada_b, ln_g, ln_b):
    splits = list(np.cumsum(IN_WIDTHS)[:-1])
    for l in range(DEPTH):
        mod = jax.nn.silu(c) @ ada_w[l] + ada_b[l]
        shift, scale, gate = jnp.split(mod, 3, axis=-1)
        h = x * (1.0 + scale[:, None, :]) + shift[:, None, :]
        p = h @ w_in[l]
        a, g_a, u, v_sg, g_b, q, k, v_sb, g_c = jnp.split(p, splits, axis=-1)
        y_a = _pool_mixer(a, pool_w[l], pool_scale[l]) * jax.nn.silu(g_a)
        y_b = _spatial_gating(u, v_sg, sgu_ln_g[l], sgu_ln_b[l], sgu_w[l], sgu_b[l]) * jax.nn.silu(g_b)
        y_c = _stick_breaking(q, k, v_sb) * jax.nn.silu(g_c)
        y = jnp.concatenate([y_a, y_b, y_c], axis=-1) @ w_out[l]
        x = _layer_norm(DN_ALPHA * x + gate[:, None, :] * y, ln_g[l], ln_b[l])
    return x
```

```python
import functools

import jax
import jax.numpy as jnp
from jax import lax
from jax.experimental import pallas as pl
from jax.experimental.pallas import tpu as pltpu

D_MODEL = 1024
HEAD_DIM = 64
POOL_WINDOWS = (2, 4, 8, 16)
POOL_WIDTH = 256
POOL_HALO = 16
SGU_WIDTH = 256
SGU_HEADS = 4
SGU_BLOCK = 128
CHUNK = 64
SB_WIDTH = 512
SB_PAIRS = SB_WIDTH // (2 * HEAD_DIM)
ATTN_BLOCK = 128
F32_WIDTH = 5 * 256 + SB_WIDTH
LN_EPS = 1e-5
LOG2E = 1.4426950408889634
LN2 = 0.6931471805599453

SEQ_TILE = 512
VMEM_LIMIT_BYTES = 52 * 1024 * 1024


def _silu(g):
    return g / (1.0 + jnp.exp(-g))


def _mod_kernel(c_ref, w_ref, b_ref, o_ref):
    c = c_ref[...]
    sc = _silu(c)
    o_ref[0] = jnp.dot(sc, w_ref[0], preferred_element_type=jnp.float32,
                       precision=lax.Precision.HIGHEST) + b_ref[0]


def _modulation(c, ada_w, ada_b):
    depth, d, n = ada_w.shape
    batch = c.shape[0]
    tn = 512
    return pl.pallas_call(
        _mod_kernel,
        grid=(depth, n // tn),
        in_specs=[
            pl.BlockSpec((batch, d), lambda l, j: (0, 0)),
            pl.BlockSpec((1, d, tn), lambda l, j: (l, 0, j)),
            pl.BlockSpec((1, 1, tn), lambda l, j: (l, 0, j)),
        ],
        out_specs=pl.BlockSpec((1, batch, tn), lambda l, j: (l, 0, j)),
        out_shape=jax.ShapeDtypeStruct((depth, batch, n), jnp.float32),
        name="adaln_mod",
    )(c, ada_w, ada_b.reshape(depth, 1, n))


def _inproj_kernel(x_ref, shift_ref, scale_ref, wf_ref, wq_ref, wkt_ref, wv_ref,
                   pf_ref, q_ref, kt_ref, v_ref):
    h = (x_ref[0] * (1.0 + scale_ref[0]) + shift_ref[0]).astype(jnp.bfloat16)
    for j in range(F32_WIDTH // 256):
        cols = slice(j * 256, (j + 1) * 256)
        pf_ref[0, :, cols] = jnp.dot(h, wf_ref[:, cols], preferred_element_type=jnp.float32)
    for p in range(SB_PAIRS):
        cols = slice(p * 128, (p + 1) * 128)
        q = jnp.dot(h, wq_ref[:, cols], preferred_element_type=jnp.float32)
        q_ref[0, p] = (q * (HEAD_DIM ** -0.5)).astype(jnp.bfloat16)
        v = jnp.dot(h, wv_ref[:, cols], preferred_element_type=jnp.float32)
        v_ref[0, p] = v.astype(jnp.bfloat16)
        kt = lax.dot_general(wkt_ref[cols, :], h, (((1,), (1,)), ((), ())),
                             preferred_element_type=jnp.float32)
        kt_ref[0, p] = kt.astype(jnp.bfloat16)


def _in_projection(x, mod3, wf, wq, wkt, wv):
    batch, seq, d = x.shape
    ts = SEQ_TILE
    full = lambda shape: pl.BlockSpec(shape, lambda b, s: (0,) * len(shape))
    return pl.pallas_call(
        _inproj_kernel,
        grid=(batch, seq // ts),
        in_specs=[
            pl.BlockSpec((1, ts, d), lambda b, s: (b, s, 0)),
            pl.BlockSpec((1, 1, d), lambda b, s: (b, 0, 0)),
            pl.BlockSpec((1, 1, d), lambda b, s: (b, 0, 1)),
            full(wf.shape), full(wq.shape), full(wkt.shape), full(wv.shape),
        ],
        out_specs=[
            pl.BlockSpec((1, ts, F32_WIDTH), lambda b, s: (b, s, 0)),
            pl.BlockSpec((1, SB_PAIRS, ts, 128), lambda b, s: (b, 0, s, 0)),
            pl.BlockSpec((1, SB_PAIRS, 128, ts), lambda b, s: (b, 0, 0, s)),
            pl.BlockSpec((1, SB_PAIRS, ts, 128), lambda b, s: (b, 0, s, 0)),
        ],
        out_shape=[
            jax.ShapeDtypeStruct((batch, seq, F32_WIDTH), jnp.float32),
            jax.ShapeDtypeStruct((batch, SB_PAIRS, seq, 128), jnp.bfloat16),
            jax.ShapeDtypeStruct((batch, SB_PAIRS, 128, seq), jnp.bfloat16),
            jax.ShapeDtypeStruct((batch, SB_PAIRS, seq, 128), jnp.bfloat16),
        ],
        compiler_params=pltpu.CompilerParams(
            dimension_semantics=("parallel", "parallel"),
            vmem_limit_bytes=VMEM_LIMIT_BYTES),
        name="in_projection",
    )(x, mod3, mod3, wf, wq, wkt, wv)


def _split_bf16(x):
    hi = x.astype(jnp.bfloat16)
    lo = (x - hi.astype(jnp.float32)).astype(jnp.bfloat16)
    return hi, lo


def _pool_mixer(a, halo, first_row, pool_w_bd, pool_scale):
    t = a.shape[0]
    ext = jnp.concatenate([halo, a], axis=0)
    lane = lax.broadcasted_iota(jnp.int32, ext.shape, 1)
    acc = ext
    total = None
    for g, win in enumerate(POOL_WINDOWS):
        acc = acc + pltpu.roll(acc, win // 2, axis=0)
        in_group = (lane >= g * 64) & (lane < (g + 1) * 64)
        total = jnp.where(in_group, acc, 0.0) if total is None else jnp.where(in_group, acc, total)
    win_sum = total[POOL_HALO:]
    row = lax.broadcasted_iota(jnp.int32, (t, POOL_WIDTH), 0) + first_row
    lane_t = lax.broadcasted_iota(jnp.int32, (t, POOL_WIDTH), 1)
    win_lane = jnp.left_shift(2, lane_t // 64)
    cnt = jnp.minimum(row + 1, win_lane).astype(jnp.float32)
    d = (win_sum / cnt - a).astype(jnp.bfloat16)
    y = jnp.dot(d, pool_w_bd, preferred_element_type=jnp.float32)
    return y * pool_scale


def _spatial_gating(u, v, ln_g, ln_b, w_masked, bias_full):
    t = u.shape[0]
    mu = jnp.mean(v, axis=-1, keepdims=True)
    var = jnp.mean(jnp.square(v - mu), axis=-1, keepdims=True)
    vn = ((v - mu) * lax.rsqrt(var + LN_EPS) * ln_g + ln_b).astype(jnp.bfloat16)
    lane = lax.broadcasted_iota(jnp.int32, (SGU_BLOCK, SGU_WIDTH), 1)
    outs = []
    for n in range(t // SGU_BLOCK):
        vb = vn[n * SGU_BLOCK:(n + 1) * SGU_BLOCK]
        mixed = bias_full
        for h in range(SGU_HEADS):
            r = jnp.dot(w_masked[h], vb, preferred_element_type=jnp.float32)
            in_head = (lane >= h * HEAD_DIM) & (lane < (h + 1) * HEAD_DIM)
            mixed = mixed + jnp.where(in_head, r, 0.0)
        outs.append(mixed)
    return u * jnp.concatenate(outs, axis=0)


def _stick_break_block(q_pair, kt_ref, v_ref, pair, q_block, tri):
    lane = lax.broadcasted_iota(jnp.int32, (ATTN_BLOCK, 128), 1)
    row = lax.broadcasted_iota(jnp.int32, (ATTN_BLOCK, ATTN_BLOCK), 0)
    col = lax.broadcasted_iota(jnp.int32, (ATTN_BLOCK, ATTN_BLOCK), 1)
    zero = jnp.zeros_like(q_pair)
    q_heads = (jnp.where(lane < HEAD_DIM, q_pair, zero), jnp.where(lane >= HEAD_DIM, q_pair, zero))

    def body(step, state):
        key_block = q_block - step
        k0 = pl.multiple_of(key_block * ATTN_BLOCK, ATTN_BLOCK)
        kt = kt_ref[0, pair, :, pl.ds(k0, ATTN_BLOCK)]
        vb = v_ref[0, pair, pl.ds(k0, ATTN_BLOCK), :]
        valid = (col - row) < step * ATTN_BLOCK
        new_state = []
        for h in range(2):
            acc, carry = state[2 * h], state[2 * h + 1]
            z2 = jnp.dot(q_heads[h], kt, preferred_element_type=jnp.float32) * LOG2E
            sp2 = jnp.maximum(z2, 0.0) + jnp.log2(1.0 + jnp.exp2(-jnp.abs(z2)))
            sp2 = jnp.where(valid, sp2, 0.0)
            hi, lo = _split_bf16(sp2)
            suffix = (jnp.dot(hi, tri, preferred_element_type=jnp.float32)
                      + jnp.dot(lo, tri, preferred_element_type=jnp.float32))
            w = jnp.where(valid, jnp.exp2(z2 - suffix - carry), 0.0)
            acc = acc + jnp.dot(w.astype(jnp.bfloat16), vb, preferred_element_type=jnp.float32)
            carry = carry + suffix[:, 0:1]
            new_state += [acc, carry]
        return tuple(new_state)

    acc0 = jnp.zeros((ATTN_BLOCK, 128), jnp.float32)
    carry0 = jnp.zeros((ATTN_BLOCK, 1), jnp.float32)
    state = lax.fori_loop(0, q_block + 1, body, (acc0, carry0, acc0, carry0))
    return jnp.where(lane < HEAD_DIM, state[0], state[2])


def _mixer_kernel(pf_ref, halo_ref, q_ref, kt_ref, v_ref, x_ref, gate_ref,
                  pool_w_ref, pool_scale_ref, sgu_g_ref, sgu_b_ref, sgu_w_ref, sgu_bias_ref,
                  wout_ref, lng_ref, lnb_ref, o_ref, y_scr, att_scr, *, alpha):
    s = pl.program_id(1)
    t = pf_ref.shape[1]

    a = pf_ref[0, :, 0:256]
    halo = jnp.where(s > 0, halo_ref[0], 0.0)
    y_a = _pool_mixer(a, halo, s * t, pool_w_ref[...], pool_scale_ref[...])
    y_scr[:, 0:256] = (y_a * _silu(pf_ref[0, :, 256:512])).astype(jnp.bfloat16)

    ti = lax.broadcasted_iota(jnp.int32, (SGU_BLOCK, SGU_BLOCK), 0)
    si = lax.broadcasted_iota(jnp.int32, (SGU_BLOCK, SGU_BLOCK), 1)
    chunk_causal = (si // CHUNK) <= (ti // CHUNK)
    w_masked = jnp.where(chunk_causal[None], sgu_w_ref[...], 0.0).astype(jnp.bfloat16)
    y_b = _spatial_gating(pf_ref[0, :, 512:768], pf_ref[0, :, 768:1024],
                          sgu_g_ref[...], sgu_b_ref[...], w_masked, sgu_bias_ref[...])
    y_scr[:, 256:512] = (y_b * _silu(pf_ref[0, :, 1024:1280])).astype(jnp.bfloat16)

    tri = (lax.broadcasted_iota(jnp.int32, (ATTN_BLOCK, ATTN_BLOCK), 0)
           >= lax.broadcasted_iota(jnp.int32, (ATTN_BLOCK, ATTN_BLOCK), 1)).astype(jnp.bfloat16)
    blocks_per_tile = t // ATTN_BLOCK

    def attn_body(i, _):
        pair = i // blocks_per_tile
        qb = i % blocks_per_tile
        r0 = pl.multiple_of(qb * ATTN_BLOCK, ATTN_BLOCK)
        q_pair = q_ref[0, pair, pl.ds(r0, ATTN_BLOCK), :]
        att_scr[pair, pl.ds(r0, ATTN_BLOCK), :] = _stick_break_block(
            q_pair, kt_ref, v_ref, pair, s * blocks_per_tile + qb, tri)
        return 0

    lax.fori_loop(0, SB_PAIRS * blocks_per_tile, attn_body, 0)
    for p in range(SB_PAIRS):
        g_c = pf_ref[0, :, 1280 + p * 128:1280 + (p + 1) * 128]
        y_scr[:, 512 + p * 128:512 + (p + 1) * 128] = (att_scr[p] * _silu(g_c)).astype(jnp.bfloat16)

    y = jnp.dot(y_scr[...], wout_ref[...], preferred_element_type=jnp.float32)
    r = alpha * x_ref[0] + gate_ref[0] * y
    mu = jnp.mean(r, axis=-1, keepdims=True)
    var = jnp.mean(jnp.square(r - mu), axis=-1, keepdims=True)
    o_ref[0] = (r - mu) * lax.rsqrt(var + LN_EPS) * lng_ref[...] + lnb_ref[...]


def _mixers(pf, q, kt, v, x, mod3, pool_w_bd, pool_scale, sgu_g, sgu_b, sgu_w, sgu_bias,
            wout, ln_g, ln_b, alpha):
    batch, seq, d = x.shape
    t = SEQ_TILE
    halo_blocks = t // POOL_HALO
    full = lambda shape: pl.BlockSpec(shape, lambda b, s: (0,) * len(shape))
    return pl.pallas_call(
        functools.partial(_mixer_kernel, alpha=alpha),
        grid=(batch, seq // t),
        in_specs=[
            pl.BlockSpec((1, t, F32_WIDTH), lambda b, s: (b, s, 0)),
            pl.BlockSpec((1, POOL_HALO, POOL_WIDTH),
                         lambda b, s: (b, jnp.maximum(s * halo_blocks - 1, 0), 0)),
            pl.BlockSpec((1, SB_PAIRS, t, 128), lambda b, s: (b, 0, s, 0)),
            pl.BlockSpec((1, SB_PAIRS, 128, seq), lambda b, s: (b, 0, 0, 0)),
            pl.BlockSpec((1, SB_PAIRS, seq, 128), lambda b, s: (b, 0, 0, 0)),
            pl.BlockSpec((1, t, d), lambda b, s: (b, s, 0)),
            pl.BlockSpec((1, 1, d), lambda b, s: (b, 0, 2)),
            full(pool_w_bd.shape), full(pool_scale.shape), full(sgu_g.shape), full(sgu_b.shape),
            full(sgu_w.shape), full(sgu_bias.shape), full(wout.shape), full(ln_g.shape),
            full(ln_b.shape),
        ],
        out_specs=pl.BlockSpec((1, t, d), lambda b, s: (b, s, 0)),
        out_shape=jax.ShapeDtypeStruct((batch, seq, d), jnp.float32),
        scratch_shapes=[
            pltpu.VMEM((t, d), jnp.bfloat16),
            pltpu.VMEM((SB_PAIRS, t, 128), jnp.float32),
        ],
        compiler_params=pltpu.CompilerParams(
            dimension_semantics=("parallel", "arbitrary"),
            vmem_limit_bytes=VMEM_LIMIT_BYTES),
        name="mixers_outproj",
    )(pf, pf, q, kt, v, x, mod3, pool_w_bd, pool_scale, sgu_g, sgu_b, sgu_w, sgu_bias,
      wout, ln_g, ln_b)


def _block_diag(w):
    g, n, _ = w.shape
    out = jnp.zeros((g * n, g * n), w.dtype)
    for i in range(g):
        out = out.at[i * n:(i + 1) * n, i * n:(i + 1) * n].set(w[i])
    return out


def kernel(x, c, w_in, pool_w, pool_scale, sgu_ln_g, sgu_ln_b, sgu_w, sgu_b, w_out, ada_w, ada_b,
           ln_g, ln_b):
    depth = w_in.shape[0]
    batch = x.shape[0]
    alpha = (2 * depth) ** 0.25
    mod = _modulation(c, ada_w, ada_b)
    bf = jnp.bfloat16
    for l in range(depth):
        w = w_in[l]
        wf = jnp.concatenate([w[:, 0:1280], w[:, 2816:3328]], axis=1).astype(bf)
        wq = w[:, 1280:1792].astype(bf)
        wkt = w[:, 1792:2304].T.astype(bf)
        wv = w[:, 2304:2816].astype(bf)
        mod3 = mod[l].reshape(batch, 1, 3 * D_MODEL)
        pf, q, kt, v = _in_projection(x, mod3, wf, wq, wkt, wv)
        x = _mixers(
            pf, q, kt, v, x, mod3,
            _block_diag(pool_w[l]).astype(bf), pool_scale[l].reshape(1, POOL_WIDTH),
            sgu_ln_g[l].reshape(1, SGU_WIDTH), sgu_ln_b[l].reshape(1, SGU_WIDTH),
            sgu_w[l], jnp.repeat(sgu_b[l].T, HEAD_DIM, axis=1),
            w_out[l].astype(bf), ln_g[l].reshape(1, D_MODEL), ln_b[l].reshape(1, D_MODEL), alpha)
    return x
```

```python
import functools

import jax
import jax.numpy as jnp
from jax import lax
from jax.experimental import pallas as pl
from jax.experimental.pallas import tpu as pltpu

D_MODEL = 1024
HEAD_DIM = 64
POOL_WINDOWS = (2, 4, 8, 16)
POOL_WIDTH = 256
POOL_HALO = 16
SGU_WIDTH = 256
SGU_HEADS = 4
SGU_BLOCK = 128
CHUNK = 64
SB_WIDTH = 512
SB_PAIRS = SB_WIDTH // (2 * HEAD_DIM)
ATTN_BLOCK = 128
ATTN_SKIP_LOG2 = 150.0
ATTN_MASKED = -1e30
ATTN_SOFTPLUS_CUTOFF = 64.0
F32_WIDTH = 5 * 256 + SB_WIDTH
LN_EPS = 1e-5
LOG2E = 1.4426950408889634
LN2 = 0.6931471805599453

SEQ_TILE = 512
VMEM_LIMIT_BYTES = 52 * 1024 * 1024


def _silu(g):
    return g / (1.0 + jnp.exp(-g))


def _mod_kernel(c_ref, w_ref, b_ref, o_ref):
    c = c_ref[...]
    sc = _silu(c)
    o_ref[0] = jnp.dot(sc, w_ref[0], preferred_element_type=jnp.float32,
                       precision=lax.Precision.HIGHEST) + b_ref[0]


def _modulation(c, ada_w, ada_b):
    depth, d, n = ada_w.shape
    batch = c.shape[0]
    tn = 512
    return pl.pallas_call(
        _mod_kernel,
        grid=(depth, n // tn),
        in_specs=[
            pl.BlockSpec((batch, d), lambda l, j: (0, 0)),
            pl.BlockSpec((1, d, tn), lambda l, j: (l, 0, j)),
            pl.BlockSpec((1, 1, tn), lambda l, j: (l, 0, j)),
        ],
        out_specs=pl.BlockSpec((1, batch, tn), lambda l, j: (l, 0, j)),
        out_shape=jax.ShapeDtypeStruct((depth, batch, n), jnp.float32),
        name="adaln_mod",
    )(c, ada_w, ada_b.reshape(depth, 1, n))


def _inproj_kernel(x_ref, shift_ref, scale_ref, wf_ref, wq_ref, wkt_ref, wv_ref,
                   pf_ref, q_ref, kt_ref, v_ref):
    h = (x_ref[0] * (1.0 + scale_ref[0]) + shift_ref[0]).astype(jnp.bfloat16)
    for j in range(F32_WIDTH // 256):
        cols = slice(j * 256, (j + 1) * 256)
        pf_ref[0, :, cols] = jnp.dot(h, wf_ref[:, cols], preferred_element_type=jnp.float32)
    t = h.shape[0]
    zero = jnp.zeros((ATTN_BLOCK, 128), jnp.bfloat16)
    lane_lo = lax.broadcasted_iota(jnp.int32, (ATTN_BLOCK, 128), 1) < HEAD_DIM
    row_lo = lax.broadcasted_iota(jnp.int32, (ATTN_BLOCK, 128), 0) < HEAD_DIM
    for p in range(SB_PAIRS):
        cols = slice(p * 128, (p + 1) * 128)
        q = jnp.dot(h, wq_ref[:, cols], preferred_element_type=jnp.float32)
        q_ref[0, p] = (q * (HEAD_DIM ** -0.5 * LOG2E)).astype(jnp.bfloat16)
        v = jnp.dot(h, wv_ref[:, cols], preferred_element_type=jnp.float32).astype(jnp.bfloat16)
        kt = lax.dot_general(wkt_ref[cols, :], h, (((1,), (1,)), ((), ())),
                             preferred_element_type=jnp.float32).astype(jnp.bfloat16)
        for j in range(t // ATTN_BLOCK):
            kb = kt[:, j * ATTN_BLOCK:(j + 1) * ATTN_BLOCK]
            kt_ref[0, p, :, 2 * j * ATTN_BLOCK:(2 * j + 1) * ATTN_BLOCK] = jnp.where(row_lo, kb, zero)
            kt_ref[0, p, :, (2 * j + 1) * ATTN_BLOCK:(2 * j + 2) * ATTN_BLOCK] = jnp.where(row_lo, zero, kb)
            vb = v[j * ATTN_BLOCK:(j + 1) * ATTN_BLOCK]
            v_ref[0, p, 2 * j * ATTN_BLOCK:(2 * j + 1) * ATTN_BLOCK, :] = jnp.where(lane_lo, vb, zero)
            v_ref[0, p, (2 * j + 1) * ATTN_BLOCK:(2 * j + 2) * ATTN_BLOCK, :] = jnp.where(lane_lo, zero, vb)


def _in_projection(x, mod3, wf, wq, wkt, wv):
    batch, seq, d = x.shape
    ts = SEQ_TILE
    full = lambda shape: pl.BlockSpec(shape, lambda b, s: (0,) * len(shape))
    return pl.pallas_call(
        _inproj_kernel,
        grid=(batch, seq // ts),
        in_specs=[
            pl.BlockSpec((1, ts, d), lambda b, s: (b, s, 0)),
            pl.BlockSpec((1, 1, d), lambda b, s: (b, 0, 0)),
            pl.BlockSpec((1, 1, d), lambda b, s: (b, 0, 1)),
            full(wf.shape), full(wq.shape), full(wkt.shape), full(wv.shape),
        ],
        out_specs=[
            pl.BlockSpec((1, ts, F32_WIDTH), lambda b, s: (b, s, 0)),
            pl.BlockSpec((1, SB_PAIRS, ts, 128), lambda b, s: (b, 0, s, 0)),
            pl.BlockSpec((1, SB_PAIRS, 128, 2 * ts), lambda b, s: (b, 0, 0, s)),
            pl.BlockSpec((1, SB_PAIRS, 2 * ts, 128), lambda b, s: (b, 0, s, 0)),
        ],
        out_shape=[
            jax.ShapeDtypeStruct((batch, seq, F32_WIDTH), jnp.float32),
            jax.ShapeDtypeStruct((batch, SB_PAIRS, seq, 128), jnp.bfloat16),
            jax.ShapeDtypeStruct((batch, SB_PAIRS, 128, 2 * seq), jnp.bfloat16),
            jax.ShapeDtypeStruct((batch, SB_PAIRS, 2 * seq, 128), jnp.bfloat16),
        ],
        compiler_params=pltpu.CompilerParams(
            dimension_semantics=("parallel", "parallel"),
            vmem_limit_bytes=VMEM_LIMIT_BYTES),
        name="in_projection",
    )(x, mod3, mod3, wf, wq, wkt, wv)


def _pool_mixer(a, halo, first_row, pool_w_bd, pool_scale):
    t = a.shape[0]
    ext = jnp.concatenate([halo, a], axis=0)
    lane = lax.broadcasted_iota(jnp.int32, ext.shape, 1)
    acc = ext
    total = None
    for g, win in enumerate(POOL_WINDOWS):
        acc = acc + pltpu.roll(acc, win // 2, axis=0)
        in_group = (lane >= g * 64) & (lane < (g + 1) * 64)
        total = jnp.where(in_group, acc, 0.0) if total is None else jnp.where(in_group, acc, total)
    win_sum = total[POOL_HALO:]
    row = lax.broadcasted_iota(jnp.int32, (t, POOL_WIDTH), 0) + first_row
    lane_t = lax.broadcasted_iota(jnp.int32, (t, POOL_WIDTH), 1)
    win_lane = jnp.left_shift(2, lane_t // 64)
    cnt = jnp.minimum(row + 1, win_lane).astype(jnp.float32)
    d = (win_sum / cnt - a).astype(jnp.bfloat16)
    y = jnp.dot(d, pool_w_bd, preferred_element_type=jnp.float32)
    return y * pool_scale


def _spatial_gating(u, v, ln_g, ln_b, w_masked, bias_full):
    t = u.shape[0]
    mu = jnp.mean(v, axis=-1, keepdims=True)
    var = jnp.mean(jnp.square(v - mu), axis=-1, keepdims=True)
    vn = ((v - mu) * lax.rsqrt(var + LN_EPS) * ln_g + ln_b).astype(jnp.bfloat16)
    lane = lax.broadcasted_iota(jnp.int32, (SGU_BLOCK, SGU_WIDTH), 1)
    outs = []
    for n in range(t // SGU_BLOCK):
        vb = vn[n * SGU_BLOCK:(n + 1) * SGU_BLOCK]
        mixed = bias_full
        for h in range(SGU_HEADS):
            r = jnp.dot(w_masked[h], vb, preferred_element_type=jnp.float32)
            in_head = (lane >= h * HEAD_DIM) & (lane < (h + 1) * HEAD_DIM)
            mixed = mixed + jnp.where(in_head, r, 0.0)
        outs.append(mixed)
    return u * jnp.concatenate(outs, axis=0)


def _attention(q_ref, kt_ref, v_ref, carry_scr, acc_scr, first_q_block, q_blocks):
    r = lax.broadcasted_iota(jnp.int32, (2 * ATTN_BLOCK, 2 * ATTN_BLOCK), 0)
    c = lax.broadcasted_iota(jnp.int32, (2 * ATTN_BLOCK, 2 * ATTN_BLOCK), 1)
    same_head = (r >= ATTN_BLOCK) == (c >= ATTN_BLOCK)
    tri2 = jnp.where(same_head & (r >= c), 1.0, 0.0).astype(jnp.bfloat16)
    row = lax.broadcasted_iota(jnp.int32, (ATTN_BLOCK, 2 * ATTN_BLOCK), 0)
    col = lax.broadcasted_iota(jnp.int32, (ATTN_BLOCK, 2 * ATTN_BLOCK), 1) & (ATTN_BLOCK - 1)
    before_query = col < row

    def tile_keys(qb, step):
        key_block = first_q_block + qb - step
        k0 = pl.multiple_of(jnp.maximum(key_block, 0) * (2 * ATTN_BLOCK), 2 * ATTN_BLOCK)
        return key_block, k0

    def logits(pair, qb, step):
        rows = slice(qb * ATTN_BLOCK, (qb + 1) * ATTN_BLOCK)
        key_block, k0 = tile_keys(qb, step)
        z2 = jnp.dot(q_ref[0, pair, rows, :], kt_ref[0, pair, :, pl.ds(k0, 2 * ATTN_BLOCK)],
                     preferred_element_type=jnp.float32)
        if isinstance(step, int):
            z2 = jnp.where(before_query, z2, ATTN_MASKED)
        else:
            z2 = z2 + jnp.where(key_block >= 0, 0.0, ATTN_MASKED)
        sp2 = jnp.where(z2 > ATTN_SOFTPLUS_CUTOFF, z2, jnp.log(1.0 + jnp.exp2(z2)) * LOG2E)
        return z2, sp2.astype(jnp.bfloat16)

    def weights(pair, qb, step, z2, spb):
        rows = slice(qb * ATTN_BLOCK, (qb + 1) * ATTN_BLOCK)
        suffix = jnp.dot(spb, tri2, preferred_element_type=jnp.float32)
        total = jnp.concatenate(
            [jnp.broadcast_to(suffix[:, h * ATTN_BLOCK:h * ATTN_BLOCK + 1], (ATTN_BLOCK, ATTN_BLOCK))
             for h in range(2)], axis=1)
        if isinstance(step, int):
            w = jnp.exp2(z2 - suffix)
            carry = total
        else:
            old = carry_scr[pair, rows, :]
            w = jnp.exp2(z2 - suffix - old)
            carry = old + total
        carry_scr[pair, rows, :] = carry
        return w.astype(jnp.bfloat16), carry

    def accumulate(pair, qb, step, w):
        rows = slice(qb * ATTN_BLOCK, (qb + 1) * ATTN_BLOCK)
        _, k0 = tile_keys(qb, step)
        out = jnp.dot(w, v_ref[0, pair, pl.ds(k0, 2 * ATTN_BLOCK), :],
                      preferred_element_type=jnp.float32)
        if isinstance(step, int):
            acc_scr[pair, rows, :] = out
        else:
            acc_scr[pair, rows, :] += out

    tiles = [(pair, qb) for qb in range(q_blocks) for pair in range(SB_PAIRS)]

    def visit(step):
        n = len(tiles)
        zs, ws, carries = {}, {}, {}
        for i in range(n + 2):
            if i < n:
                zs[i] = logits(*tiles[i], step)
            if 0 <= i - 1 < n:
                ws[i - 1], carries[i - 1] = weights(*tiles[i - 1], step, *zs.pop(i - 1))
            if 0 <= i - 2 < n:
                accumulate(*tiles[i - 2], step, ws.pop(i - 2))
        smallest = None
        for qb in range(q_blocks):
            m = None
            for i, (_, tile_qb) in enumerate(tiles):
                if tile_qb != qb:
                    continue
                cm = jnp.minimum(carries[i][:, :ATTN_BLOCK], carries[i][:, ATTN_BLOCK:])
                m = cm if m is None else jnp.minimum(m, cm)
            m = jnp.min(m, axis=0, keepdims=True)
            has_more_keys = first_q_block + qb - step > 0
            m = jnp.where(has_more_keys, m, ATTN_SKIP_LOG2)
            smallest = m if smallest is None else jnp.minimum(smallest, m)
        return jnp.min(smallest) < ATTN_SKIP_LOG2

    def body(state):
        step, _ = state
        return step + 1, visit(step)

    lax.while_loop(lambda state: state[1], body, (jnp.int32(1), visit(0)))


def _mixer_kernel(pf_ref, halo_ref, q_ref, kt_ref, v_ref, x_ref, gate_ref,
                  pool_w_ref, pool_scale_ref, sgu_g_ref, sgu_b_ref, sgu_w_ref, sgu_bias_ref,
                  wout_ref, lng_ref, lnb_ref, o_ref, y_scr, att_scr, carry_scr, *, alpha):
    s = pl.program_id(1)
    t = pf_ref.shape[1]

    a = pf_ref[0, :, 0:256]
    halo = jnp.where(s > 0, halo_ref[0], 0.0)
    y_a = _pool_mixer(a, halo, s * t, pool_w_ref[...], pool_scale_ref[...])
    y_scr[:, 0:256] = (y_a * _silu(pf_ref[0, :, 256:512])).astype(jnp.bfloat16)

    ti = lax.broadcasted_iota(jnp.int32, (SGU_BLOCK, SGU_BLOCK), 0)
    si = lax.broadcasted_iota(jnp.int32, (SGU_BLOCK, SGU_BLOCK), 1)
    chunk_causal = (si // CHUNK) <= (ti // CHUNK)
    w_masked = jnp.where(chunk_causal[None], sgu_w_ref[...], 0.0).astype(jnp.bfloat16)
    y_b = _spatial_gating(pf_ref[0, :, 512:768], pf_ref[0, :, 768:1024],
                          sgu_g_ref[...], sgu_b_ref[...], w_masked, sgu_bias_ref[...])
    y_scr[:, 256:512] = (y_b * _silu(pf_ref[0, :, 1024:1280])).astype(jnp.bfloat16)

    blocks_per_tile = t // ATTN_BLOCK
    _attention(q_ref, kt_ref, v_ref, carry_scr, att_scr, s * blocks_per_tile, blocks_per_tile)
    for p in range(SB_PAIRS):
        g_c = pf_ref[0, :, 1280 + p * 128:1280 + (p + 1) * 128]
        y_scr[:, 512 + p * 128:512 + (p + 1) * 128] = (att_scr[p] * _silu(g_c)).astype(jnp.bfloat16)

    y = jnp.dot(y_scr[...], wout_ref[...], preferred_element_type=jnp.float32)
    r = alpha * x_ref[0] + gate_ref[0] * y
    mu = jnp.mean(r, axis=-1, keepdims=True)
    var = jnp.mean(jnp.square(r - mu), axis=-1, keepdims=True)
    o_ref[0] = (r - mu) * lax.rsqrt(var + LN_EPS) * lng_ref[...] + lnb_ref[...]


def _mixers(pf, q, kt, v, x, mod3, pool_w_bd, pool_scale, sgu_g, sgu_b, sgu_w, sgu_bias,
            wout, ln_g, ln_b, alpha):
    batch, seq, d = x.shape
    t = SEQ_TILE
    halo_blocks = t // POOL_HALO
    full = lambda shape: pl.BlockSpec(shape, lambda b, s: (0,) * len(shape))
    return pl.pallas_call(
        functools.partial(_mixer_kernel, alpha=alpha),
        grid=(batch, seq // t),
        in_specs=[
            pl.BlockSpec((1, t, F32_WIDTH), lambda b, s: (b, s, 0)),
            pl.BlockSpec((1, POOL_HALO, POOL_WIDTH),
                         lambda b, s: (b, jnp.maximum(s * halo_blocks - 1, 0), 0)),
            pl.BlockSpec((1, SB_PAIRS, t, 128), lambda b, s: (b, 0, s, 0)),
            pl.BlockSpec((1, SB_PAIRS, 128, 2 * seq), lambda b, s: (b, 0, 0, 0)),
            pl.BlockSpec((1, SB_PAIRS, 2 * seq, 128), lambda b, s: (b, 0, 0, 0)),
            pl.BlockSpec((1, t, d), lambda b, s: (b, s, 0)),
            pl.BlockSpec((1, 1, d), lambda b, s: (b, 0, 2)),
            full(pool_w_bd.shape), full(pool_scale.shape), full(sgu_g.shape), full(sgu_b.shape),
            full(sgu_w.shape), full(sgu_bias.shape), full(wout.shape), full(ln_g.shape),
            full(ln_b.shape),
        ],
        out_specs=pl.BlockSpec((1, t, d), lambda b, s: (b, s, 0)),
        out_shape=jax.ShapeDtypeStruct((batch, seq, d), jnp.float32),
        scratch_shapes=[
            pltpu.VMEM((t, d), jnp.bfloat16),
            pltpu.VMEM((SB_PAIRS, t, 128), jnp.float32),
            pltpu.VMEM((SB_PAIRS, t, 2 * ATTN_BLOCK), jnp.float32),
        ],
        compiler_params=pltpu.CompilerParams(
            dimension_semantics=("parallel", "arbitrary"),
            vmem_limit_bytes=VMEM_LIMIT_BYTES),
        name="mixers_outproj",
    )(pf, pf, q, kt, v, x, mod3, pool_w_bd, pool_scale, sgu_g, sgu_b, sgu_w, sgu_bias,
      wout, ln_g, ln_b)


def _block_diag(w):
    g, n, _ = w.shape
    out = jnp.zeros((g * n, g * n), w.dtype)
    for i in range(g):
        out = out.at[i * n:(i + 1) * n, i * n:(i + 1) * n].set(w[i])
    return out


def kernel(x, c, w_in, pool_w, pool_scale, sgu_ln_g, sgu_ln_b, sgu_w, sgu_b, w_out, ada_w, ada_b,
           ln_g, ln_b):
    depth = w_in.shape[0]
    batch = x.shape[0]
    alpha = (2 * depth) ** 0.25
    mod = _modulation(c, ada_w, ada_b)
    bf = jnp.bfloat16
    for l in range(depth):
        w = w_in[l]
        wf = jnp.concatenate([w[:, 0:1280], w[:, 2816:3328]], axis=1).astype(bf)
        wq = w[:, 1280:1792].astype(bf)
        wkt = w[:, 1792:2304].T.astype(bf)
        wv = w[:, 2304:2816].astype(bf)
        mod3 = mod[l].reshape(batch, 1, 3 * D_MODEL)
        pf, q, kt, v = _in_projection(x, mod3, wf, wq, wkt, wv)
        x = _mixers(
            pf, q, kt, v, x, mod3,
            _block_diag(pool_w[l]).astype(bf), pool_scale[l].reshape(1, POOL_WIDTH),
            sgu_ln_g[l].reshape(1, SGU_WIDTH), sgu_ln_b[l].reshape(1, SGU_WIDTH),
            sgu_w[l], jnp.repeat(sgu_b[l].T, HEAD_DIM, axis=1),
            w_out[l].astype(bf), ln_g[l].reshape(1, D_MODEL), ln_b[l].reshape(1, D_MODEL), alpha)
    return x
```

```python
import functools

import jax
import jax.numpy as jnp
from jax import lax
from jax.experimental import pallas as pl
from jax.experimental.pallas import tpu as pltpu

D_MODEL = 1024
HEAD_DIM = 64
POOL_WINDOWS = (2, 4, 8, 16)
POOL_WIDTH = 256
POOL_HALO = 16
SGU_WIDTH = 256
SGU_HEADS = 4
SGU_BLOCK = 128
CHUNK = 64
SB_WIDTH = 512
SB_PAIRS = SB_WIDTH // (2 * HEAD_DIM)
ATTN_BLOCK = 128
ATTN_GROUP = 256
ATTN_STAGE_LAGS = (2, 4)
ATTN_SKIP_LOG2 = 150.0
ATTN_MASKED = -1e30
ATTN_SOFTPLUS_CUTOFF = 64.0
F32_WIDTH = 5 * 256 + SB_WIDTH
LN_EPS = 1e-5
LOG2E = 1.4426950408889634
LN2 = 0.6931471805599453

SEQ_TILE = 512
VMEM_LIMIT_BYTES = 52 * 1024 * 1024


def _silu(g):
    return g / (1.0 + jnp.exp(-g))


def _mod_kernel(c_ref, w_ref, b_ref, o_ref):
    c = c_ref[...]
    sc = _silu(c)
    o_ref[0] = jnp.dot(sc, w_ref[0], preferred_element_type=jnp.float32,
                       precision=lax.Precision.HIGHEST) + b_ref[0]


def _modulation(c, ada_w, ada_b):
    depth, d, n = ada_w.shape
    batch = c.shape[0]
    tn = 512
    return pl.pallas_call(
        _mod_kernel,
        grid=(depth, n // tn),
        in_specs=[
            pl.BlockSpec((batch, d), lambda l, j: (0, 0)),
            pl.BlockSpec((1, d, tn), lambda l, j: (l, 0, j)),
            pl.BlockSpec((1, 1, tn), lambda l, j: (l, 0, j)),
        ],
        out_specs=pl.BlockSpec((1, batch, tn), lambda l, j: (l, 0, j)),
        out_shape=jax.ShapeDtypeStruct((depth, batch, n), jnp.float32),
        name="adaln_mod",
    )(c, ada_w, ada_b.reshape(depth, 1, n))


def _inproj_kernel(x_ref, shift_ref, scale_ref, wf_ref, wqt_ref, wk_ref, wvt_ref,
                   pf_ref, qt_ref, k_ref, vt_ref):
    h = (x_ref[0] * (1.0 + scale_ref[0]) + shift_ref[0]).astype(jnp.bfloat16)
    for j in range(F32_WIDTH // 256):
        cols = slice(j * 256, (j + 1) * 256)
        pf_ref[0, :, cols] = jnp.dot(h, wf_ref[:, cols], preferred_element_type=jnp.float32)
    t = h.shape[0]
    zero = jnp.zeros((ATTN_BLOCK, 128), jnp.bfloat16)
    lane_lo = lax.broadcasted_iota(jnp.int32, (ATTN_BLOCK, 128), 1) < HEAD_DIM
    row_lo = lax.broadcasted_iota(jnp.int32, (ATTN_BLOCK, 128), 0) < HEAD_DIM
    nt = (((1,), (1,)), ((), ()))
    qt_all = lax.dot_general(wqt_ref[...], h, nt, preferred_element_type=jnp.float32)
    qt_all = (qt_all * (HEAD_DIM ** -0.5 * LOG2E)).astype(jnp.bfloat16)
    k_all = jnp.dot(h, wk_ref[...], preferred_element_type=jnp.float32).astype(jnp.bfloat16)
    vt_all = lax.dot_general(wvt_ref[...], h, nt,
                             preferred_element_type=jnp.float32).astype(jnp.bfloat16)
    for p in range(SB_PAIRS):
        cols = slice(p * 128, (p + 1) * 128)
        qt_ref[0, p] = qt_all[cols]
        k = k_all[:, cols]
        vt = vt_all[cols]
        for j in range(t // ATTN_BLOCK):
            kb = k[j * ATTN_BLOCK:(j + 1) * ATTN_BLOCK]
            k_ref[0, p, 2 * j * ATTN_BLOCK:(2 * j + 1) * ATTN_BLOCK, :] = jnp.where(lane_lo, kb, zero)
            k_ref[0, p, (2 * j + 1) * ATTN_BLOCK:(2 * j + 2) * ATTN_BLOCK, :] = jnp.where(lane_lo, zero, kb)
            vb = vt[:, j * ATTN_BLOCK:(j + 1) * ATTN_BLOCK]
            vt_ref[0, p, :, 2 * j * ATTN_BLOCK:(2 * j + 1) * ATTN_BLOCK] = jnp.where(row_lo, vb, zero)
            vt_ref[0, p, :, (2 * j + 1) * ATTN_BLOCK:(2 * j + 2) * ATTN_BLOCK] = jnp.where(row_lo, zero, vb)


def _in_projection(x, mod3, wf, wqt, wk, wvt):
    batch, seq, d = x.shape
    ts = SEQ_TILE
    full = lambda shape: pl.BlockSpec(shape, lambda b, s: (0,) * len(shape))
    return pl.pallas_call(
        _inproj_kernel,
        grid=(batch, seq // ts),
        in_specs=[
            pl.BlockSpec((1, ts, d), lambda b, s: (b, s, 0)),
            pl.BlockSpec((1, 1, d), lambda b, s: (b, 0, 0)),
            pl.BlockSpec((1, 1, d), lambda b, s: (b, 0, 1)),
            full(wf.shape), full(wqt.shape), full(wk.shape), full(wvt.shape),
        ],
        out_specs=[
            pl.BlockSpec((1, ts, F32_WIDTH), lambda b, s: (b, s, 0)),
            pl.BlockSpec((1, SB_PAIRS, 128, ts), lambda b, s: (b, 0, 0, s)),
            pl.BlockSpec((1, SB_PAIRS, 2 * ts, 128), lambda b, s: (b, 0, s, 0)),
            pl.BlockSpec((1, SB_PAIRS, 128, 2 * ts), lambda b, s: (b, 0, 0, s)),
        ],
        out_shape=[
            jax.ShapeDtypeStruct((batch, seq, F32_WIDTH), jnp.float32),
            jax.ShapeDtypeStruct((batch, SB_PAIRS, 128, seq), jnp.bfloat16),
            jax.ShapeDtypeStruct((batch, SB_PAIRS, 2 * seq, 128), jnp.bfloat16),
            jax.ShapeDtypeStruct((batch, SB_PAIRS, 128, 2 * seq), jnp.bfloat16),
        ],
        compiler_params=pltpu.CompilerParams(
            dimension_semantics=("parallel", "parallel"),
            vmem_limit_bytes=VMEM_LIMIT_BYTES),
        name="in_projection",
    )(x, mod3, mod3, wf, wqt, wk, wvt)


def _pool_mixer(a, halo, first_row, pool_w_bd, pool_scale):
    t = a.shape[0]
    ext = jnp.concatenate([halo, a], axis=0)
    lane = lax.broadcasted_iota(jnp.int32, ext.shape, 1)
    acc = ext
    total = None
    for g, win in enumerate(POOL_WINDOWS):
        acc = acc + pltpu.roll(acc, win // 2, axis=0)
        in_group = (lane >= g * 64) & (lane < (g + 1) * 64)
        total = jnp.where(in_group, acc, 0.0) if total is None else jnp.where(in_group, acc, total)
    win_sum = total[POOL_HALO:]
    row = lax.broadcasted_iota(jnp.int32, (t, POOL_WIDTH), 0) + first_row
    lane_t = lax.broadcasted_iota(jnp.int32, (t, POOL_WIDTH), 1)
    win_lane = jnp.left_shift(2, lane_t // 64)
    cnt = jnp.minimum(row + 1, win_lane).astype(jnp.float32)
    d = (win_sum / cnt - a).astype(jnp.bfloat16)
    y = jnp.dot(d, pool_w_bd, preferred_element_type=jnp.float32)
    return y * pool_scale


def _spatial_gating(u, v, ln_g, ln_b, w_masked, bias_full):
    t = u.shape[0]
    mu = jnp.mean(v, axis=-1, keepdims=True)
    var = jnp.mean(jnp.square(v - mu), axis=-1, keepdims=True)
    vn = ((v - mu) * lax.rsqrt(var + LN_EPS) * ln_g + ln_b).astype(jnp.bfloat16)
    lane = lax.broadcasted_iota(jnp.int32, (SGU_BLOCK, SGU_WIDTH), 1)
    outs = []
    for n in range(t // SGU_BLOCK):
        vb = vn[n * SGU_BLOCK:(n + 1) * SGU_BLOCK]
        mixed = bias_full
        for h in range(SGU_HEADS):
            r = jnp.dot(w_masked[h], vb, preferred_element_type=jnp.float32)
            in_head = (lane >= h * HEAD_DIM) & (lane < (h + 1) * HEAD_DIM)
            mixed = mixed + jnp.where(in_head, r, 0.0)
        outs.append(mixed)
    return u * jnp.concatenate(outs, axis=0)


def _attention(qt_ref, k_ref, vt_ref, carry_scr, acc_scr, first_group, groups):
    two = 2 * ATTN_BLOCK
    r = lax.broadcasted_iota(jnp.int32, (two, two), 0)
    c = lax.broadcasted_iota(jnp.int32, (two, two), 1)
    same_head = (r >= ATTN_BLOCK) == (c >= ATTN_BLOCK)
    tri2 = jnp.where(same_head & (c >= r), 1.0, 0.0).astype(jnp.bfloat16)
    blocks_per_group = ATTN_GROUP // ATTN_BLOCK
    key = lax.broadcasted_iota(jnp.int32, (two, ATTN_GROUP), 0) & (ATTN_BLOCK - 1)
    query = lax.broadcasted_iota(jnp.int32, (two, ATTN_GROUP), 1)
    causal = {t: key + (blocks_per_group - 1 - t) * ATTN_BLOCK < query for t in range(blocks_per_group)}

    def tile_keys(group, step):
        key_block = (first_group + group + 1) * blocks_per_group - 1 - step
        k0 = pl.multiple_of(jnp.maximum(key_block, 0) * two, two)
        return key_block, k0

    def logits(pair, group, step):
        cols = slice(group * ATTN_GROUP, (group + 1) * ATTN_GROUP)
        _, k0 = tile_keys(group, step)
        z2 = jnp.dot(k_ref[0, pair, pl.ds(k0, two), :], qt_ref[0, pair, :, cols],
                     preferred_element_type=jnp.float32)
        if isinstance(step, int):
            z2 = jnp.where(causal[step], z2, ATTN_MASKED)
        sp2 = jnp.where(z2 > ATTN_SOFTPLUS_CUTOFF, z2, jnp.log(1.0 + jnp.exp2(z2)) * LOG2E)
        return z2, sp2.astype(jnp.bfloat16)

    def weights(pair, group, step, z2, spb):
        cols = slice(group * ATTN_GROUP, (group + 1) * ATTN_GROUP)
        key_block, _ = tile_keys(group, step)
        suffix = jnp.dot(tri2, spb, preferred_element_type=jnp.float32)
        ws, carries = [], []
        for h in range(2):
            rows = slice(h * ATTN_BLOCK, (h + 1) * ATTN_BLOCK)
            total = suffix[h * ATTN_BLOCK:h * ATTN_BLOCK + 1, :]
            x = z2[rows] - suffix[rows]
            if isinstance(step, int) and step == 0:
                carry = total
            else:
                old = carry_scr[pair, h:h + 1, cols]
                if not isinstance(step, int):
                    old = old + jnp.where(key_block >= 0, 0.0, -ATTN_MASKED)
                x = x - old
                carry = old + total
            carry_scr[pair, h:h + 1, cols] = carry
            ws.append(jnp.exp2(x).astype(jnp.bfloat16))
            carries.append(carry)
        return jnp.concatenate(ws, axis=0), jnp.minimum(carries[0], carries[1])

    def accumulate(pair, group, step, w):
        cols = slice(group * ATTN_GROUP, (group + 1) * ATTN_GROUP)
        _, k0 = tile_keys(group, step)
        out = jnp.dot(vt_ref[0, pair, :, pl.ds(k0, two)], w,
                      preferred_element_type=jnp.float32)
        if isinstance(step, int) and step == 0:
            acc_scr[pair, :, cols] = out
        else:
            acc_scr[pair, :, cols] += out

    tiles = [(pair, group) for group in range(groups) for pair in range(SB_PAIRS)]

    def visit(step):
        n = len(tiles)
        zs, ws, carries = {}, {}, {}
        lag_w, lag_a = ATTN_STAGE_LAGS
        for i in range(n + lag_a):
            if i < n:
                zs[i] = logits(*tiles[i], step)
            if 0 <= i - lag_w < n:
                ws[i - lag_w], carries[i - lag_w] = weights(*tiles[i - lag_w], step, *zs.pop(i - lag_w))
            if 0 <= i - lag_a < n:
                accumulate(*tiles[i - lag_a], step, ws.pop(i - lag_a))
        smallest = None
        for i, (_, group) in enumerate(tiles):
            key_block, _ = tile_keys(group, step)
            m = jnp.where(key_block > 0, carries[i], ATTN_SKIP_LOG2)
            smallest = m if smallest is None else jnp.minimum(smallest, m)
        return jnp.min(smallest) < ATTN_SKIP_LOG2

    def body(state):
        step, _ = state
        return step + 1, visit(step)

    for masked_step in range(blocks_per_group):
        live = visit(masked_step)
    lax.while_loop(lambda state: state[1], body, (jnp.int32(blocks_per_group), live))


def _mixer_kernel(pf_ref, halo_ref, qt_ref, k_ref, vt_ref, x_ref, gate_ref,
                  pool_w_ref, pool_scale_ref, sgu_g_ref, sgu_b_ref, sgu_w_ref, sgu_bias_ref,
                  wout_ref, lng_ref, lnb_ref, o_ref, y_scr, att_scr, carry_scr, *, alpha):
    s = pl.program_id(1)
    t = pf_ref.shape[1]

    a = pf_ref[0, :, 0:256]
    halo = jnp.where(s > 0, halo_ref[0], 0.0)
    y_a = _pool_mixer(a, halo, s * t, pool_w_ref[...], pool_scale_ref[...])
    y_scr[:, 0:256] = (y_a * _silu(pf_ref[0, :, 256:512])).astype(jnp.bfloat16)

    ti = lax.broadcasted_iota(jnp.int32, (SGU_BLOCK, SGU_BLOCK), 0)
    si = lax.broadcasted_iota(jnp.int32, (SGU_BLOCK, SGU_BLOCK), 1)
    chunk_causal = (si // CHUNK) <= (ti // CHUNK)
    w_masked = jnp.where(chunk_causal[None], sgu_w_ref[...], 0.0).astype(jnp.bfloat16)
    y_b = _spatial_gating(pf_ref[0, :, 512:768], pf_ref[0, :, 768:1024],
                          sgu_g_ref[...], sgu_b_ref[...], w_masked, sgu_bias_ref[...])
    y_scr[:, 256:512] = (y_b * _silu(pf_ref[0, :, 1024:1280])).astype(jnp.bfloat16)

    groups_per_tile = t // ATTN_GROUP
    _attention(qt_ref, k_ref, vt_ref, carry_scr, att_scr, s * groups_per_tile, groups_per_tile)
    for p in range(SB_PAIRS):
        g_c = pf_ref[0, :, 1280 + p * 128:1280 + (p + 1) * 128]
        y_scr[:, 512 + p * 128:512 + (p + 1) * 128] = (att_scr[p].T * _silu(g_c)).astype(jnp.bfloat16)

    y = jnp.dot(y_scr[...], wout_ref[...], preferred_element_type=jnp.float32)
    r = alpha * x_ref[0] + gate_ref[0] * y
    mu = jnp.mean(r, axis=-1, keepdims=True)
    var = jnp.mean(jnp.square(r - mu), axis=-1, keepdims=True)
    o_ref[0] = (r - mu) * lax.rsqrt(var + LN_EPS) * lng_ref[...] + lnb_ref[...]


def _mixers(pf, qt, k, vt, x, mod3, pool_w_bd, pool_scale, sgu_g, sgu_b, sgu_w, sgu_bias,
            wout, ln_g, ln_b, alpha):
    batch, seq, d = x.shape
    t = SEQ_TILE
    halo_blocks = t // POOL_HALO
    full = lambda shape: pl.BlockSpec(shape, lambda b, s: (0,) * len(shape))
    return pl.pallas_call(
        functools.partial(_mixer_kernel, alpha=alpha),
        grid=(batch, seq // t),
        in_specs=[
            pl.BlockSpec((1, t, F32_WIDTH), lambda b, s: (b, s, 0)),
            pl.BlockSpec((1, POOL_HALO, POOL_WIDTH),
                         lambda b, s: (b, jnp.maximum(s * halo_blocks - 1, 0), 0)),
            pl.BlockSpec((1, SB_PAIRS, 128, t), lambda b, s: (b, 0, 0, s)),
            pl.BlockSpec((1, SB_PAIRS, 2 * seq, 128), lambda b, s: (b, 0, 0, 0)),
            pl.BlockSpec((1, SB_PAIRS, 128, 2 * seq), lambda b, s: (b, 0, 0, 0)),
            pl.BlockSpec((1, t, d), lambda b, s: (b, s, 0)),
            pl.BlockSpec((1, 1, d), lambda b, s: (b, 0, 2)),
            full(pool_w_bd.shape), full(pool_scale.shape), full(sgu_g.shape), full(sgu_b.shape),
            full(sgu_w.shape), full(sgu_bias.shape), full(wout.shape), full(ln_g.shape),
            full(ln_b.shape),
        ],
        out_specs=pl.BlockSpec((1, t, d), lambda b, s: (b, s, 0)),
        out_shape=jax.ShapeDtypeStruct((batch, seq, d), jnp.float32),
        scratch_shapes=[
            pltpu.VMEM((t, d), jnp.bfloat16),
            pltpu.VMEM((SB_PAIRS, 128, t), jnp.float32),
            pltpu.VMEM((SB_PAIRS, 2, t), jnp.float32),
        ],
        compiler_params=pltpu.CompilerParams(
            dimension_semantics=("parallel", "arbitrary"),
            vmem_limit_bytes=VMEM_LIMIT_BYTES),
        name="mixers_outproj",
    )(pf, pf, qt, k, vt, x, mod3, pool_w_bd, pool_scale, sgu_g, sgu_b, sgu_w, sgu_bias,
      wout, ln_g, ln_b)


def _block_diag(w):
    g, n, _ = w.shape
    out = jnp.zeros((g * n, g * n), w.dtype)
    for i in range(g):
        out = out.at[i * n:(i + 1) * n, i * n:(i + 1) * n].set(w[i])
    return out


def kernel(x, c, w_in, pool_w, pool_scale, sgu_ln_g, sgu_ln_b, sgu_w, sgu_b, w_out, ada_w, ada_b,
           ln_g, ln_b):
    depth = w_in.shape[0]
    batch = x.shape[0]
    alpha = (2 * depth) ** 0.25
    mod = _modulation(c, ada_w, ada_b)
    bf = jnp.bfloat16
    for l in range(depth):
        w = w_in[l]
        wf = jnp.concatenate([w[:, 0:1280], w[:, 2816:3328]], axis=1).astype(bf)
        wqt = w[:, 1280:1792].T.astype(bf)
        wk = w[:, 1792:2304].astype(bf)
        wvt = w[:, 2304:2816].T.astype(bf)
        mod3 = mod[l].reshape(batch, 1, 3 * D_MODEL)
        pf, qt, k, vt = _in_projection(x, mod3, wf, wqt, wk, wvt)
        x = _mixers(
            pf, qt, k, vt, x, mod3,
            _block_diag(pool_w[l]).astype(bf), pool_scale[l].reshape(1, POOL_WIDTH),
            sgu_ln_g[l].reshape(1, SGU_WIDTH), sgu_ln_b[l].reshape(1, SGU_WIDTH),
            sgu_w[l], jnp.repeat(sgu_b[l].T, HEAD_DIM, axis=1),
            w_out[l].astype(bf), ln_g[l].reshape(1, D_MODEL), ln_b[l].reshape(1, D_MODEL), alpha)
    return x
```

```python
import functools

import jax
import jax.numpy as jnp
from jax import lax
from jax.experimental import pallas as pl
from jax.experimental.pallas import tpu as pltpu

D_MODEL = 1024
HEAD_DIM = 64
POOL_WINDOWS = (2, 4, 8, 16)
POOL_WIDTH = 256
POOL_HALO = 16
SGU_WIDTH = 256
SGU_HEADS = 4
SGU_BLOCK = 128
CHUNK = 64
SB_WIDTH = 512
SB_PAIRS = SB_WIDTH // (2 * HEAD_DIM)
ATTN_BLOCK = 128
ATTN_GROUP = 256
ATTN_STAGE_LAGS = (3, 6)
ATTN_STEPS_PER_VISIT = 2
ATTN_SKIP = 104.0
ATTN_MASKED = -1e30
F32_WIDTH = 5 * 256 + SB_WIDTH
LN_EPS = 1e-5

SEQ_TILE = 512
OUT_ROWS = 256
VMEM_LIMIT_BYTES = 52 * 1024 * 1024


def _silu(g):
    return g / (1.0 + jnp.exp(-g))


def _mod_kernel(c_ref, w_ref, b_ref, o_ref):
    c = c_ref[...]
    sc = _silu(c)
    o_ref[0] = jnp.dot(sc, w_ref[0], preferred_element_type=jnp.float32,
                       precision=lax.Precision.HIGHEST) + b_ref[0]


def _modulation(c, ada_w, ada_b):
    depth, d, n = ada_w.shape
    batch = c.shape[0]
    tn = 512
    return pl.pallas_call(
        _mod_kernel,
        grid=(depth, n // tn),
        in_specs=[
            pl.BlockSpec((batch, d), lambda l, j: (0, 0)),
            pl.BlockSpec((1, d, tn), lambda l, j: (l, 0, j)),
            pl.BlockSpec((1, 1, tn), lambda l, j: (l, 0, j)),
        ],
        out_specs=pl.BlockSpec((1, batch, tn), lambda l, j: (l, 0, j)),
        out_shape=jax.ShapeDtypeStruct((depth, batch, n), jnp.float32),
        name="adaln_mod",
    )(c, ada_w, ada_b.reshape(depth, 1, n))


def _inproj_kernel(x_ref, shift_ref, scale_ref, wf_ref, wqt_ref, wk_ref, wvt_ref,
                   pf_ref, qt_ref, k_ref, vt_ref):
    h = (x_ref[0] * (1.0 + scale_ref[0]) + shift_ref[0]).astype(jnp.bfloat16)
    for j in range(F32_WIDTH // 256):
        cols = slice(j * 256, (j + 1) * 256)
        pf_ref[0, :, cols] = jnp.dot(h, wf_ref[:, cols], preferred_element_type=jnp.float32)
    t = h.shape[0]
    zero = jnp.zeros((ATTN_BLOCK, 128), jnp.bfloat16)
    lane_lo = lax.broadcasted_iota(jnp.int32, (ATTN_BLOCK, 128), 1) < HEAD_DIM
    row_lo = lax.broadcasted_iota(jnp.int32, (ATTN_BLOCK, 128), 0) < HEAD_DIM
    nt = (((1,), (1,)), ((), ()))
    qt_all = lax.dot_general(wqt_ref[...], h, nt, preferred_element_type=jnp.float32)
    qt_all = (qt_all * HEAD_DIM ** -0.5).astype(jnp.bfloat16)
    k_all = jnp.dot(h, wk_ref[...], preferred_element_type=jnp.float32).astype(jnp.bfloat16)
    vt_all = lax.dot_general(wvt_ref[...], h, nt,
                             preferred_element_type=jnp.float32).astype(jnp.bfloat16)
    for p in range(SB_PAIRS):
        cols = slice(p * 128, (p + 1) * 128)
        qt_ref[0, p] = qt_all[cols]
        k = k_all[:, cols]
        vt = vt_all[cols]
        for j in range(t // ATTN_BLOCK):
            kb = k[j * ATTN_BLOCK:(j + 1) * ATTN_BLOCK]
            k_ref[0, p, 2 * j * ATTN_BLOCK:(2 * j + 1) * ATTN_BLOCK, :] = jnp.where(lane_lo, kb, zero)
            k_ref[0, p, (2 * j + 1) * ATTN_BLOCK:(2 * j + 2) * ATTN_BLOCK, :] = jnp.where(lane_lo, zero, kb)
            vb = vt[:, j * ATTN_BLOCK:(j + 1) * ATTN_BLOCK]
            vt_ref[0, p, :, 2 * j * ATTN_BLOCK:(2 * j + 1) * ATTN_BLOCK] = jnp.where(row_lo, vb, zero)
            vt_ref[0, p, :, (2 * j + 1) * ATTN_BLOCK:(2 * j + 2) * ATTN_BLOCK] = jnp.where(row_lo, zero, vb)


def _in_projection(x, mod3, wf, wqt, wk, wvt):
    batch, seq, d = x.shape
    ts = SEQ_TILE
    full = lambda shape: pl.BlockSpec(shape, lambda b, s: (0,) * len(shape))
    return pl.pallas_call(
        _inproj_kernel,
        grid=(batch, seq // ts),
        in_specs=[
            pl.BlockSpec((1, ts, d), lambda b, s: (b, s, 0)),
            pl.BlockSpec((1, 1, d), lambda b, s: (b, 0, 0)),
            pl.BlockSpec((1, 1, d), lambda b, s: (b, 0, 1)),
            full(wf.shape), full(wqt.shape), full(wk.shape), full(wvt.shape),
        ],
        out_specs=[
            pl.BlockSpec((1, ts, F32_WIDTH), lambda b, s: (b, s, 0)),
            pl.BlockSpec((1, SB_PAIRS, 128, ts), lambda b, s: (b, 0, 0, s)),
            pl.BlockSpec((1, SB_PAIRS, 2 * ts, 128), lambda b, s: (b, 0, s, 0)),
            pl.BlockSpec((1, SB_PAIRS, 128, 2 * ts), lambda b, s: (b, 0, 0, s)),
        ],
        out_shape=[
            jax.ShapeDtypeStruct((batch, seq, F32_WIDTH), jnp.float32),
            jax.ShapeDtypeStruct((batch, SB_PAIRS, 128, seq), jnp.bfloat16),
            jax.ShapeDtypeStruct((batch, SB_PAIRS, 2 * seq, 128), jnp.bfloat16),
            jax.ShapeDtypeStruct((batch, SB_PAIRS, 128, 2 * seq), jnp.bfloat16),
        ],
        compiler_params=pltpu.CompilerParams(
            dimension_semantics=("parallel", "parallel"),
            vmem_limit_bytes=VMEM_LIMIT_BYTES),
        name="in_projection",
    )(x, mod3, mod3, wf, wqt, wk, wvt)


def _pool_mixer(a, halo, first_row, pool_w_bd, pool_scale):
    t = a.shape[0]
    ext = jnp.concatenate([halo, a], axis=0)
    lane = lax.broadcasted_iota(jnp.int32, ext.shape, 1)
    acc = ext
    total = None
    for g, win in enumerate(POOL_WINDOWS):
        acc = acc + pltpu.roll(acc, win // 2, axis=0)
        in_group = (lane >= g * 64) & (lane < (g + 1) * 64)
        total = jnp.where(in_group, acc, 0.0) if total is None else jnp.where(in_group, acc, total)
    win_sum = total[POOL_HALO:]
    row = lax.broadcasted_iota(jnp.int32, (t, POOL_WIDTH), 0) + first_row
    lane_t = lax.broadcasted_iota(jnp.int32, (t, POOL_WIDTH), 1)
    win_lane = jnp.left_shift(2, lane_t // 64)
    cnt = jnp.minimum(row + 1, win_lane).astype(jnp.float32)
    d = (win_sum / cnt - a).astype(jnp.bfloat16)
    y = jnp.dot(d, pool_w_bd, preferred_element_type=jnp.float32)
    return y * pool_scale


def _spatial_gating(u, v, ln_g, ln_b, w_masked, bias_full):
    t = u.shape[0]
    mu = jnp.mean(v, axis=-1, keepdims=True)
    var = jnp.mean(jnp.square(v - mu), axis=-1, keepdims=True)
    vn = ((v - mu) * lax.rsqrt(var + LN_EPS) * ln_g + ln_b).astype(jnp.bfloat16)
    lane = lax.broadcasted_iota(jnp.int32, (SGU_BLOCK, SGU_WIDTH), 1)
    outs = []
    for n in range(t // SGU_BLOCK):
        vb = vn[n * SGU_BLOCK:(n + 1) * SGU_BLOCK]
        mixed = bias_full
        for h in range(SGU_HEADS):
            r = jnp.dot(w_masked[h], vb, preferred_element_type=jnp.float32)
            in_head = (lane >= h * HEAD_DIM) & (lane < (h + 1) * HEAD_DIM)
            mixed = mixed + jnp.where(in_head, r, 0.0)
        outs.append(mixed)
    return u * jnp.concatenate(outs, axis=0)


def _attention(qt_ref, k_ref, vt_ref, carry_scr, acc_scr, first_group, groups):
    two = 2 * ATTN_BLOCK
    r = lax.broadcasted_iota(jnp.int32, (two, two), 0)
    c = lax.broadcasted_iota(jnp.int32, (two, two), 1)
    same_head = (r >= ATTN_BLOCK) == (c >= ATTN_BLOCK)
    tri2 = jnp.where(same_head & (c >= r), 1.0, 0.0).astype(jnp.bfloat16)
    blocks_per_group = ATTN_GROUP // ATTN_BLOCK
    key = lax.broadcasted_iota(jnp.int32, (two, ATTN_GROUP), 0) & (ATTN_BLOCK - 1)
    query = lax.broadcasted_iota(jnp.int32, (two, ATTN_GROUP), 1)
    causal = {t: key + (blocks_per_group - 1 - t) * ATTN_BLOCK < query for t in range(blocks_per_group)}

    def tile_keys(group, step):
        key_block = (first_group + group + 1) * blocks_per_group - 1 - step
        k0 = pl.multiple_of(jnp.maximum(key_block, 0) * two, two)
        return key_block, k0

    def logits(pair, group, step):
        cols = slice(group * ATTN_GROUP, (group + 1) * ATTN_GROUP)
        _, k0 = tile_keys(group, step)
        z = jnp.dot(k_ref[0, pair, pl.ds(k0, two), :], qt_ref[0, pair, :, cols],
                    preferred_element_type=jnp.float32)
        if isinstance(step, int):
            z = jnp.where(causal[step], z, ATTN_MASKED)
        zb = z.astype(jnp.bfloat16)
        return z, jnp.maximum(zb, 0.0) + jnp.log(1.0 + jnp.exp(-jnp.abs(zb)))

    def weights(pair, group, step, z, spb):
        cols = slice(group * ATTN_GROUP, (group + 1) * ATTN_GROUP)
        key_block, _ = tile_keys(group, step)
        suffix = jnp.dot(tri2, spb, preferred_element_type=jnp.float32)
        ws, carries = [], []
        for h in range(2):
            rows = slice(h * ATTN_BLOCK, (h + 1) * ATTN_BLOCK)
            total = suffix[h * ATTN_BLOCK:h * ATTN_BLOCK + 1, :]
            x = z[rows] - suffix[rows]
            if isinstance(step, int) and step == 0:
                carry = total
            else:
                old = carry_scr[pair, h:h + 1, cols]
                if not isinstance(step, int):
                    old = old + jnp.where(key_block >= 0, 0.0, -ATTN_MASKED)
                x = x - old
                carry = old + total
            carry_scr[pair, h:h + 1, cols] = carry
            ws.append(jnp.exp(x).astype(jnp.bfloat16))
            carries.append(carry)
        return jnp.concatenate(ws, axis=0), jnp.minimum(carries[0], carries[1])

    def accumulate(pair, group, step, w):
        cols = slice(group * ATTN_GROUP, (group + 1) * ATTN_GROUP)
        _, k0 = tile_keys(group, step)
        out = jnp.dot(vt_ref[0, pair, :, pl.ds(k0, two)], w,
                      preferred_element_type=jnp.float32)
        if isinstance(step, int) and step == 0:
            acc_scr[pair, :, cols] = out
        else:
            acc_scr[pair, :, cols] += out

    tiles = [(pair, group) for group in range(groups) for pair in range(SB_PAIRS)]

    def visit(steps):
        work = [(pair, group, step) for step in steps for pair, group in tiles]
        n = len(work)
        zs, ws, carries = {}, {}, {}
        lag_w, lag_a = ATTN_STAGE_LAGS
        for i in range(n + lag_a):
            if i < n:
                zs[i] = logits(*work[i])
            if 0 <= i - lag_w < n:
                ws[i - lag_w], carries[i - lag_w] = weights(*work[i - lag_w], *zs.pop(i - lag_w))
            if 0 <= i - lag_a < n:
                accumulate(*work[i - lag_a], ws.pop(i - lag_a))
        smallest = None
        for i in range(n - len(tiles), n):
            _, group, step = work[i]
            key_block, _ = tile_keys(group, step)
            m = jnp.where(key_block > 0, carries[i], ATTN_SKIP)
            smallest = m if smallest is None else jnp.minimum(smallest, m)
        return jnp.min(smallest) < ATTN_SKIP

    def body(state):
        step, _ = state
        return step + ATTN_STEPS_PER_VISIT, visit([step + i for i in range(ATTN_STEPS_PER_VISIT)])

    live = visit(list(range(blocks_per_group)))
    lax.while_loop(lambda state: state[1], body, (jnp.int32(blocks_per_group), live))


def _mixer_kernel(pf_ref, halo_ref, qt_ref, k_ref, vt_ref, x_ref, gate_ref,
                  pool_w_ref, pool_scale_ref, sgu_g_ref, sgu_b_ref, sgu_w_ref, sgu_bias_ref,
                  wout_ref, lng_ref, lnb_ref, o_ref, y_scr, att_scr, carry_scr, *, alpha):
    s = pl.program_id(1)
    t = pf_ref.shape[1]

    a = pf_ref[0, :, 0:256]
    halo = jnp.where(s > 0, halo_ref[0], 0.0)
    y_a = _pool_mixer(a, halo, s * t, pool_w_ref[...], pool_scale_ref[...])
    y_scr[:, 0:256] = (y_a * _silu(pf_ref[0, :, 256:512])).astype(jnp.bfloat16)

    ti = lax.broadcasted_iota(jnp.int32, (SGU_BLOCK, SGU_BLOCK), 0)
    si = lax.broadcasted_iota(jnp.int32, (SGU_BLOCK, SGU_BLOCK), 1)
    chunk_causal = (si // CHUNK) <= (ti // CHUNK)
    w_masked = jnp.where(chunk_causal[None], sgu_w_ref[...], 0.0).astype(jnp.bfloat16)
    y_b = _spatial_gating(pf_ref[0, :, 512:768], pf_ref[0, :, 768:1024],
                          sgu_g_ref[...], sgu_b_ref[...], w_masked, sgu_bias_ref[...])
    y_scr[:, 256:512] = (y_b * _silu(pf_ref[0, :, 1024:1280])).astype(jnp.bfloat16)

    groups_per_tile = t // ATTN_GROUP
    _attention(qt_ref, k_ref, vt_ref, carry_scr, att_scr, s * groups_per_tile, groups_per_tile)
    for p in range(SB_PAIRS):
        g_c = pf_ref[0, :, 1280 + p * 128:1280 + (p + 1) * 128]
        y_scr[:, 512 + p * 128:512 + (p + 1) * 128] = (att_scr[p].T * _silu(g_c)).astype(jnp.bfloat16)

    for i in range(t // OUT_ROWS):
        rows = slice(i * OUT_ROWS, (i + 1) * OUT_ROWS)
        y = jnp.dot(y_scr[rows, :], wout_ref[...], preferred_element_type=jnp.float32)
        r = alpha * x_ref[0, rows, :] + gate_ref[0] * y
        mu = jnp.mean(r, axis=-1, keepdims=True)
        var = jnp.mean(jnp.square(r - mu), axis=-1, keepdims=True)
        o_ref[0, rows, :] = (r - mu) * lax.rsqrt(var + LN_EPS) * lng_ref[...] + lnb_ref[...]


def _mixers(pf, qt, k, vt, x, mod3, pool_w_bd, pool_scale, sgu_g, sgu_b, sgu_w, sgu_bias,
            wout, ln_g, ln_b, alpha):
    batch, seq, d = x.shape
    t = SEQ_TILE
    halo_blocks = t // POOL_HALO
    full = lambda shape: pl.BlockSpec(shape, lambda b, s: (0,) * len(shape))
    return pl.pallas_call(
        functools.partial(_mixer_kernel, alpha=alpha),
        grid=(batch, seq // t),
        in_specs=[
            pl.BlockSpec((1, t, F32_WIDTH), lambda b, s: (b, s, 0)),
            pl.BlockSpec((1, POOL_HALO, POOL_WIDTH),
                         lambda b, s: (b, jnp.maximum(s * halo_blocks - 1, 0), 0)),
            pl.BlockSpec((1, SB_PAIRS, 128, t), lambda b, s: (b, 0, 0, s)),
            pl.BlockSpec((1, SB_PAIRS, 2 * seq, 128), lambda b, s: (b, 0, 0, 0)),
            pl.BlockSpec((1, SB_PAIRS, 128, 2 * seq), lambda b, s: (b, 0, 0, 0)),
            pl.BlockSpec((1, t, d), lambda b, s: (b, s, 0)),
            pl.BlockSpec((1, 1, d), lambda b, s: (b, 0, 2)),
            full(pool_w_bd.shape), full(pool_scale.shape), full(sgu_g.shape), full(sgu_b.shape),
            full(sgu_w.shape), full(sgu_bias.shape), full(wout.shape), full(ln_g.shape),
            full(ln_b.shape),
        ],
        out_specs=pl.BlockSpec((1, t, d), lambda b, s: (b, s, 0)),
        out_shape=jax.ShapeDtypeStruct((batch, seq, d), jnp.float32),
        scratch_shapes=[
            pltpu.VMEM((t, d), jnp.bfloat16),
            pltpu.VMEM((SB_PAIRS, 128, t), jnp.float32),
            pltpu.VMEM((SB_PAIRS, 2, t), jnp.float32),
        ],
        compiler_params=pltpu.CompilerParams(
            dimension_semantics=("parallel", "arbitrary"),
            vmem_limit_bytes=VMEM_LIMIT_BYTES),
        name="mixers_outproj",
    )(pf, pf, qt, k, vt, x, mod3, pool_w_bd, pool_scale, sgu_g, sgu_b, sgu_w, sgu_bias,
      wout, ln_g, ln_b)


def _block_diag(w):
    g, n, _ = w.shape
    out = jnp.zeros((g * n, g * n), w.dtype)
    for i in range(g):
        out = out.at[i * n:(i + 1) * n, i * n:(i + 1) * n].set(w[i])
    return out


def kernel(x, c, w_in, pool_w, pool_scale, sgu_ln_g, sgu_ln_b, sgu_w, sgu_b, w_out, ada_w, ada_b,
           ln_g, ln_b):
    depth = w_in.shape[0]
    batch = x.shape[0]
    alpha = (2 * depth) ** 0.25
    mod = _modulation(c, ada_w, ada_b)
    bf = jnp.bfloat16
    for l in range(depth):
        w = w_in[l]
        wf = jnp.concatenate([w[:, 0:1280], w[:, 2816:3328]], axis=1).astype(bf)
        wqt = w[:, 1280:1792].T.astype(bf)
        wk = w[:, 1792:2304].astype(bf)
        wvt = w[:, 2304:2816].T.astype(bf)
        mod3 = mod[l].reshape(batch, 1, 3 * D_MODEL)
        pf, qt, k, vt = _in_projection(x, mod3, wf, wqt, wk, wvt)
        x = _mixers(
            pf, qt, k, vt, x, mod3,
            _block_diag(pool_w[l]).astype(bf), pool_scale[l].reshape(1, POOL_WIDTH),
            sgu_ln_g[l].reshape(1, SGU_WIDTH), sgu_ln_b[l].reshape(1, SGU_WIDTH),
            sgu_w[l], jnp.repeat(sgu_b[l].T, HEAD_DIM, axis=1),
            w_out[l].astype(bf), ln_g[l].reshape(1, D_MODEL), ln_b[l].reshape(1, D_MODEL), alpha)
    return x
```

```python
import functools

import jax
import jax.numpy as jnp
from jax import lax
from jax.experimental import pallas as pl
from jax.experimental.pallas import tpu as pltpu

D_MODEL = 1024
HEAD_DIM = 64
POOL_WINDOWS = (2, 4, 8, 16)
POOL_WIDTH = 256
POOL_HALO = 16
SGU_WIDTH = 256
SGU_HEADS = 4
SGU_BLOCK = 128
CHUNK = 64
SB_WIDTH = 512
SB_PAIRS = SB_WIDTH // (2 * HEAD_DIM)
ATTN_BLOCK = 128
ATTN_GROUP = 256
ATTN_STAGE_LAGS = (3, 6)
ATTN_STEPS_PER_VISIT = 2
ATTN_SKIP = 104.0
ATTN_MASKED = -1e30
LN_EPS = 1e-5

SEQ_TILE = 512
OUT_ROWS = 256
VMEM_LIMIT_BYTES = 52 * 1024 * 1024


def _silu(g):
    return g / (1.0 + jnp.exp(-g))


def _mod_kernel(c_ref, w_ref, b_ref, o_ref):
    c = c_ref[...]
    sc = _silu(c)
    o_ref[0] = jnp.dot(sc, w_ref[0], preferred_element_type=jnp.float32,
                       precision=lax.Precision.HIGHEST) + b_ref[0]


def _modulation(c, ada_w, ada_b):
    depth, d, n = ada_w.shape
    batch = c.shape[0]
    tn = 512
    return pl.pallas_call(
        _mod_kernel,
        grid=(depth, n // tn),
        in_specs=[
            pl.BlockSpec((batch, d), lambda l, j: (0, 0)),
            pl.BlockSpec((1, d, tn), lambda l, j: (l, 0, j)),
            pl.BlockSpec((1, 1, tn), lambda l, j: (l, 0, j)),
        ],
        out_specs=pl.BlockSpec((1, batch, tn), lambda l, j: (l, 0, j)),
        out_shape=jax.ShapeDtypeStruct((depth, batch, n), jnp.float32),
        name="adaln_mod",
    )(c, ada_w, ada_b.reshape(depth, 1, n))


def _inproj_kernel(x_ref, shift_ref, scale_ref, wf_ref, wqt_ref, wk_ref, wvt_ref,
                   pool_w_ref, pool_scale_ref, sgu_g_ref, sgu_b_ref, sgu_w_ref, sgu_bias_ref,
                   yab_ref, gc_ref, qt_ref, k_ref, vt_ref, halo_scr, pf_scr):
    s = pl.program_id(1)
    h = (x_ref[0] * (1.0 + scale_ref[0]) + shift_ref[0]).astype(jnp.bfloat16)
    t = h.shape[0]
    proj = lambda j: jnp.dot(h, wf_ref[:, j * 256:(j + 1) * 256], preferred_element_type=jnp.float32)
    for j in range(5):
        pf_scr[:, j * 256:(j + 1) * 256] = proj(j)
    gc_ref[0, :, 0:256] = proj(5)
    gc_ref[0, :, 256:512] = proj(6)

    a = pf_scr[:, 0:256]
    halo = jnp.where(s > 0, halo_scr[...], 0.0)
    halo_scr[...] = a[t - POOL_HALO:]
    y_a = _pool_mixer(a, halo, s * t, pool_w_ref[...], pool_scale_ref[...])
    yab_ref[0, :, 0:256] = (y_a * _silu(pf_scr[:, 256:512])).astype(jnp.bfloat16)

    ti = lax.broadcasted_iota(jnp.int32, (SGU_BLOCK, SGU_BLOCK), 0)
    si = lax.broadcasted_iota(jnp.int32, (SGU_BLOCK, SGU_BLOCK), 1)
    chunk_causal = (si // CHUNK) <= (ti // CHUNK)
    w_masked = jnp.where(chunk_causal[None], sgu_w_ref[...], 0.0).astype(jnp.bfloat16)
    y_b = _spatial_gating(pf_scr[:, 512:768], pf_scr[:, 768:1024],
                          sgu_g_ref[...], sgu_b_ref[...], w_masked, sgu_bias_ref[...])
    yab_ref[0, :, 256:512] = (y_b * _silu(pf_scr[:, 1024:1280])).astype(jnp.bfloat16)

    zero = jnp.zeros((ATTN_BLOCK, 128), jnp.bfloat16)
    lane_lo = lax.broadcasted_iota(jnp.int32, (ATTN_BLOCK, 128), 1) < HEAD_DIM
    row_lo = lax.broadcasted_iota(jnp.int32, (ATTN_BLOCK, 128), 0) < HEAD_DIM
    nt = (((1,), (1,)), ((), ()))
    qt_all = lax.dot_general(wqt_ref[...], h, nt, preferred_element_type=jnp.float32)
    qt_all = (qt_all * HEAD_DIM ** -0.5).astype(jnp.bfloat16)
    k_all = jnp.dot(h, wk_ref[...], preferred_element_type=jnp.float32).astype(jnp.bfloat16)
    vt_all = lax.dot_general(wvt_ref[...], h, nt,
                             preferred_element_type=jnp.float32).astype(jnp.bfloat16)
    for p in range(SB_PAIRS):
        cols = slice(p * 128, (p + 1) * 128)
        qt_ref[0, p] = qt_all[cols]
        k = k_all[:, cols]
        vt = vt_all[cols]
        for j in range(t // ATTN_BLOCK):
            kb = k[j * ATTN_BLOCK:(j + 1) * ATTN_BLOCK]
            k_ref[0, p, 2 * j * ATTN_BLOCK:(2 * j + 1) * ATTN_BLOCK, :] = jnp.where(lane_lo, kb, zero)
            k_ref[0, p, (2 * j + 1) * ATTN_BLOCK:(2 * j + 2) * ATTN_BLOCK, :] = jnp.where(lane_lo, zero, kb)
            vb = vt[:, j * ATTN_BLOCK:(j + 1) * ATTN_BLOCK]
            vt_ref[0, p, :, 2 * j * ATTN_BLOCK:(2 * j + 1) * ATTN_BLOCK] = jnp.where(row_lo, vb, zero)
            vt_ref[0, p, :, (2 * j + 1) * ATTN_BLOCK:(2 * j + 2) * ATTN_BLOCK] = jnp.where(row_lo, zero, vb)


def _in_projection(x, mod3, wf, wqt, wk, wvt, pool_w_bd, pool_scale, sgu_g, sgu_b, sgu_w, sgu_bias):
    batch, seq, d = x.shape
    ts = SEQ_TILE
    full = lambda shape: pl.BlockSpec(shape, lambda b, s: (0,) * len(shape))
    small = (pool_w_bd, pool_scale, sgu_g, sgu_b, sgu_w, sgu_bias)
    return pl.pallas_call(
        _inproj_kernel,
        grid=(batch, seq // ts),
        in_specs=[
            pl.BlockSpec((1, ts, d), lambda b, s: (b, s, 0)),
            pl.BlockSpec((1, 1, d), lambda b, s: (b, 0, 0)),
            pl.BlockSpec((1, 1, d), lambda b, s: (b, 0, 1)),
            full(wf.shape), full(wqt.shape), full(wk.shape), full(wvt.shape),
            *[full(w.shape) for w in small],
        ],
        out_specs=[
            pl.BlockSpec((1, ts, POOL_WIDTH + SGU_WIDTH), lambda b, s: (b, s, 0)),
            pl.BlockSpec((1, ts, SB_WIDTH), lambda b, s: (b, s, 0)),
            pl.BlockSpec((1, SB_PAIRS, 128, ts), lambda b, s: (b, 0, 0, s)),
            pl.BlockSpec((1, SB_PAIRS, 2 * ts, 128), lambda b, s: (b, 0, s, 0)),
            pl.BlockSpec((1, SB_PAIRS, 128, 2 * ts), lambda b, s: (b, 0, 0, s)),
        ],
        out_shape=[
            jax.ShapeDtypeStruct((batch, seq, POOL_WIDTH + SGU_WIDTH), jnp.bfloat16),
            jax.ShapeDtypeStruct((batch, seq, SB_WIDTH), jnp.float32),
            jax.ShapeDtypeStruct((batch, SB_PAIRS, 128, seq), jnp.bfloat16),
            jax.ShapeDtypeStruct((batch, SB_PAIRS, 2 * seq, 128), jnp.bfloat16),
            jax.ShapeDtypeStruct((batch, SB_PAIRS, 128, 2 * seq), jnp.bfloat16),
        ],
        scratch_shapes=[pltpu.VMEM((POOL_HALO, POOL_WIDTH), jnp.float32),
                        pltpu.VMEM((ts, 5 * 256), jnp.float32)],
        compiler_params=pltpu.CompilerParams(
            dimension_semantics=("parallel", "arbitrary"),
            vmem_limit_bytes=VMEM_LIMIT_BYTES),
        name="in_projection",
    )(x, mod3, mod3, wf, wqt, wk, wvt, *small)


def _pool_mixer(a, halo, first_row, pool_w_bd, pool_scale):
    t = a.shape[0]
    ext = jnp.concatenate([halo, a], axis=0)
    lane = lax.broadcasted_iota(jnp.int32, ext.shape, 1)
    acc = ext
    total = None
    for g, win in enumerate(POOL_WINDOWS):
        acc = acc + pltpu.roll(acc, win // 2, axis=0)
        in_group = (lane >= g * 64) & (lane < (g + 1) * 64)
        total = jnp.where(in_group, acc, 0.0) if total is None else jnp.where(in_group, acc, total)
    win_sum = total[POOL_HALO:]
    row = lax.broadcasted_iota(jnp.int32, (t, POOL_WIDTH), 0) + first_row
    lane_t = lax.broadcasted_iota(jnp.int32, (t, POOL_WIDTH), 1)
    win_lane = jnp.left_shift(2, lane_t // 64)
    cnt = jnp.minimum(row + 1, win_lane).astype(jnp.float32)
    d = (win_sum / cnt - a).astype(jnp.bfloat16)
    y = jnp.dot(d, pool_w_bd, preferred_element_type=jnp.float32)
    return y * pool_scale


def _spatial_gating(u, v, ln_g, ln_b, w_masked, bias_full):
    t = u.shape[0]
    mu = jnp.mean(v, axis=-1, keepdims=True)
    var = jnp.mean(jnp.square(v - mu), axis=-1, keepdims=True)
    vn = ((v - mu) * lax.rsqrt(var + LN_EPS) * ln_g + ln_b).astype(jnp.bfloat16)
    lane = lax.broadcasted_iota(jnp.int32, (SGU_BLOCK, SGU_WIDTH), 1)
    outs = []
    for n in range(t // SGU_BLOCK):
        vb = vn[n * SGU_BLOCK:(n + 1) * SGU_BLOCK]
        mixed = bias_full
        for h in range(SGU_HEADS):
            r = jnp.dot(w_masked[h], vb, preferred_element_type=jnp.float32)
            in_head = (lane >= h * HEAD_DIM) & (lane < (h + 1) * HEAD_DIM)
            mixed = mixed + jnp.where(in_head, r, 0.0)
        outs.append(mixed)
    return u * jnp.concatenate(outs, axis=0)


def _attention(qt_ref, k_ref, vt_ref, carry_scr, acc_scr, first_group, groups):
    two = 2 * ATTN_BLOCK
    r = lax.broadcasted_iota(jnp.int32, (two, two), 0)
    c = lax.broadcasted_iota(jnp.int32, (two, two), 1)
    same_head = (r >= ATTN_BLOCK) == (c >= ATTN_BLOCK)
    tri2 = jnp.where(same_head & (c >= r), 1.0, 0.0).astype(jnp.bfloat16)
    blocks_per_group = ATTN_GROUP // ATTN_BLOCK
    key = lax.broadcasted_iota(jnp.int32, (two, ATTN_GROUP), 0) & (ATTN_BLOCK - 1)
    query = lax.broadcasted_iota(jnp.int32, (two, ATTN_GROUP), 1)
    causal = {t: key + (blocks_per_group - 1 - t) * ATTN_BLOCK < query for t in range(blocks_per_group)}

    def tile_keys(group, step):
        key_block = (first_group + group + 1) * blocks_per_group - 1 - step
        k0 = pl.multiple_of(jnp.maximum(key_block, 0) * two, two)
        return key_block, k0

    def logits(pair, group, step):
        cols = slice(group * ATTN_GROUP, (group + 1) * ATTN_GROUP)
        _, k0 = tile_keys(group, step)
        z = jnp.dot(k_ref[0, pair, pl.ds(k0, two), :], qt_ref[0, pair, :, cols],
                    preferred_element_type=jnp.float32)
        if isinstance(step, int):
            z = jnp.where(causal[step], z, ATTN_MASKED)
        zb = z.astype(jnp.bfloat16)
        return z, jnp.maximum(zb, 0.0) + jnp.log(1.0 + jnp.exp(-jnp.abs(zb)))

    def weights(pair, group, step, z, spb):
        cols = slice(group * ATTN_GROUP, (group + 1) * ATTN_GROUP)
        key_block, _ = tile_keys(group, step)
        suffix = jnp.dot(tri2, spb, preferred_element_type=jnp.float32)
        ws, carries = [], []
        for h in range(2):
            rows = slice(h * ATTN_BLOCK, (h + 1) * ATTN_BLOCK)
            total = suffix[h * ATTN_BLOCK:h * ATTN_BLOCK + 1, :]
            x = z[rows] - suffix[rows]
            if isinstance(step, int) and step == 0:
                carry = total
            else:
                old = carry_scr[pair, h:h + 1, cols]
                if not isinstance(step, int):
                    old = old + jnp.where(key_block >= 0, 0.0, -ATTN_MASKED)
                x = x - old
                carry = old + total
            carry_scr[pair, h:h + 1, cols] = carry
            ws.append(jnp.exp(x).astype(jnp.bfloat16))
            carries.append(carry)
        return jnp.concatenate(ws, axis=0), jnp.minimum(carries[0], carries[1])

    def accumulate(pair, group, step, w):
        cols = slice(group * ATTN_GROUP, (group + 1) * ATTN_GROUP)
        _, k0 = tile_keys(group, step)
        out = jnp.dot(vt_ref[0, pair, :, pl.ds(k0, two)], w,
                      preferred_element_type=jnp.float32)
        if isinstance(step, int) and step == 0:
            acc_scr[pair, :, cols] = out
        else:
            acc_scr[pair, :, cols] += out

    tiles = [(pair, group) for group in range(groups) for pair in range(SB_PAIRS)]

    def visit(steps):
        work = [(pair, group, step) for step in steps for pair, group in tiles]
        n = len(work)
        zs, ws, carries = {}, {}, {}
        lag_w, lag_a = ATTN_STAGE_LAGS
        for i in range(n + lag_a):
            if i < n:
                zs[i] = logits(*work[i])
            if 0 <= i - lag_w < n:
                ws[i - lag_w], carries[i - lag_w] = weights(*work[i - lag_w], *zs.pop(i - lag_w))
            if 0 <= i - lag_a < n:
                accumulate(*work[i - lag_a], ws.pop(i - lag_a))
        smallest = None
        for i in range(n - len(tiles), n):
            _, group, step = work[i]
            key_block, _ = tile_keys(group, step)
            m = jnp.where(key_block > 0, carries[i], ATTN_SKIP)
            smallest = m if smallest is None else jnp.minimum(smallest, m)
        return jnp.min(smallest) < ATTN_SKIP

    def body(state):
        step, _ = state
        return step + ATTN_STEPS_PER_VISIT, visit([step + i for i in range(ATTN_STEPS_PER_VISIT)])

    live = visit(list(range(blocks_per_group)))
    lax.while_loop(lambda state: state[1], body, (jnp.int32(blocks_per_group), live))


def _mixer_kernel(yab_ref, gc_ref, qt_ref, k_ref, vt_ref, x_ref, gate_ref,
                  wout_ref, lng_ref, lnb_ref, o_ref, y_scr, att_scr, carry_scr, *, alpha):
    s = pl.program_id(1)
    t = x_ref.shape[1]

    groups_per_tile = t // ATTN_GROUP
    _attention(qt_ref, k_ref, vt_ref, carry_scr, att_scr, s * groups_per_tile, groups_per_tile)
    y_scr[:, 0:POOL_WIDTH + SGU_WIDTH] = yab_ref[0]
    for p in range(SB_PAIRS):
        g_c = gc_ref[0, :, p * 128:(p + 1) * 128]
        y_scr[:, 512 + p * 128:512 + (p + 1) * 128] = (att_scr[p].T * _silu(g_c)).astype(jnp.bfloat16)

    for i in range(t // OUT_ROWS):
        rows = slice(i * OUT_ROWS, (i + 1) * OUT_ROWS)
        y = jnp.dot(y_scr[rows, :], wout_ref[...], preferred_element_type=jnp.float32)
        r = alpha * x_ref[0, rows, :] + gate_ref[0] * y
        mu = jnp.mean(r, axis=-1, keepdims=True)
        var = jnp.mean(jnp.square(r - mu), axis=-1, keepdims=True)
        o_ref[0, rows, :] = (r - mu) * lax.rsqrt(var + LN_EPS) * lng_ref[...] + lnb_ref[...]


def _mixers(yab, gc, qt, k, vt, x, mod3, wout, ln_g, ln_b, alpha):
    batch, seq, d = x.shape
    t = SEQ_TILE
    full = lambda shape: pl.BlockSpec(shape, lambda b, s: (0,) * len(shape))
    return pl.pallas_call(
        functools.partial(_mixer_kernel, alpha=alpha),
        grid=(batch, seq // t),
        in_specs=[
            pl.BlockSpec((1, t, POOL_WIDTH + SGU_WIDTH), lambda b, s: (b, s, 0)),
            pl.BlockSpec((1, t, SB_WIDTH), lambda b, s: (b, s, 0)),
            pl.BlockSpec((1, SB_PAIRS, 128, t), lambda b, s: (b, 0, 0, s)),
            pl.BlockSpec((1, SB_PAIRS, 2 * seq, 128), lambda b, s: (b, 0, 0, 0)),
            pl.BlockSpec((1, SB_PAIRS, 128, 2 * seq), lambda b, s: (b, 0, 0, 0)),
            pl.BlockSpec((1, t, d), lambda b, s: (b, s, 0)),
            pl.BlockSpec((1, 1, d), lambda b, s: (b, 0, 2)),
            full(wout.shape), full(ln_g.shape), full(ln_b.shape),
        ],
        out_specs=pl.BlockSpec((1, t, d), lambda b, s: (b, s, 0)),
        out_shape=jax.ShapeDtypeStruct((batch, seq, d), jnp.float32),
        scratch_shapes=[
            pltpu.VMEM((t, d), jnp.bfloat16),
            pltpu.VMEM((SB_PAIRS, 128, t), jnp.float32),
            pltpu.VMEM((SB_PAIRS, 2, t), jnp.float32),
        ],
        compiler_params=pltpu.CompilerParams(
            dimension_semantics=("parallel", "arbitrary"),
            vmem_limit_bytes=VMEM_LIMIT_BYTES),
        name="mixers_outproj",
    )(yab, gc, qt, k, vt, x, mod3, wout, ln_g, ln_b)


def _block_diag(w):
    g, n, _ = w.shape
    out = jnp.zeros((g * n, g * n), w.dtype)
    for i in range(g):
        out = out.at[i * n:(i + 1) * n, i * n:(i + 1) * n].set(w[i])
    return out


def kernel(x, c, w_in, pool_w, pool_scale, sgu_ln_g, sgu_ln_b, sgu_w, sgu_b, w_out, ada_w, ada_b,
           ln_g, ln_b):
    depth = w_in.shape[0]
    batch = x.shape[0]
    alpha = (2 * depth) ** 0.25
    mod = _modulation(c, ada_w, ada_b)
    bf = jnp.bfloat16
    for l in range(depth):
        w = w_in[l]
        wf = jnp.concatenate([w[:, 0:1280], w[:, 2816:3328]], axis=1).astype(bf)
        wqt = w[:, 1280:1792].T.astype(bf)
        wk = w[:, 1792:2304].astype(bf)
        wvt = w[:, 2304:2816].T.astype(bf)
        mod3 = mod[l].reshape(batch, 1, 3 * D_MODEL)
        yab, gc, qt, k, vt = _in_projection(
            x, mod3, wf, wqt, wk, wvt,
            _block_diag(pool_w[l]).astype(bf), pool_scale[l].reshape(1, POOL_WIDTH),
            sgu_ln_g[l].reshape(1, SGU_WIDTH), sgu_ln_b[l].reshape(1, SGU_WIDTH),
            sgu_w[l], jnp.repeat(sgu_b[l].T, HEAD_DIM, axis=1))
        x = _mixers(yab, gc, qt, k, vt, x, mod3, w_out[l].astype(bf),
                    ln_g[l].reshape(1, D_MODEL), ln_b[l].reshape(1, D_MODEL), alpha)
    return x
```

```python
import functools

import jax
import jax.numpy as jnp
from jax import lax
from jax.experimental import pallas as pl
from jax.experimental.pallas import tpu as pltpu

D_MODEL = 1024
HEAD_DIM = 64
POOL_WINDOWS = (2, 4, 8, 16)
POOL_WIDTH = 256
POOL_HALO = 16
SGU_WIDTH = 256
SGU_HEADS = 4
SGU_BLOCK = 128
CHUNK = 64
SB_WIDTH = 512
SB_PAIRS = SB_WIDTH // (2 * HEAD_DIM)
ATTN_BLOCK = 128
ATTN_GROUP = 256
ATTN_STAGE_LAGS = (3, 6)
ATTN_STEPS_PER_VISIT = 2
ATTN_FIRST_VISIT_STEPS = 4
ATTN_SKIP = 104.0
ATTN_MASKED = -1e30
LN_EPS = 1e-5

SEQ_TILE = 512
OUT_ROWS = 256
VMEM_LIMIT_BYTES = 52 * 1024 * 1024


def _silu(g):
    return g / (1.0 + jnp.exp(-g))


def _mod_kernel(c_ref, w_ref, b_ref, o_ref):
    c = c_ref[...]
    sc = _silu(c)
    o_ref[0] = jnp.dot(sc, w_ref[0], preferred_element_type=jnp.float32,
                       precision=lax.Precision.HIGHEST) + b_ref[0]


def _modulation(c, ada_w, ada_b):
    depth, d, n = ada_w.shape
    batch = c.shape[0]
    tn = 512
    return pl.pallas_call(
        _mod_kernel,
        grid=(depth, n // tn),
        in_specs=[
            pl.BlockSpec((batch, d), lambda l, j: (0, 0)),
            pl.BlockSpec((1, d, tn), lambda l, j: (l, 0, j)),
            pl.BlockSpec((1, 1, tn), lambda l, j: (l, 0, j)),
        ],
        out_specs=pl.BlockSpec((1, batch, tn), lambda l, j: (l, 0, j)),
        out_shape=jax.ShapeDtypeStruct((depth, batch, n), jnp.float32),
        name="adaln_mod",
    )(c, ada_w, ada_b.reshape(depth, 1, n))


def _inproj_kernel(x_ref, shift_ref, scale_ref, wf_ref, wqt_ref, wk_ref, wvt_ref,
                   pool_w_ref, pool_scale_ref, sgu_g_ref, sgu_b_ref, sgu_w_ref, sgu_bias_ref,
                   yab_ref, sgc_ref, qt_ref, k_ref, vt_ref, halo_scr, pf_scr):
    s = pl.program_id(1)
    h = (x_ref[0] * (1.0 + scale_ref[0]) + shift_ref[0]).astype(jnp.bfloat16)
    t = h.shape[0]
    proj = lambda j: jnp.dot(h, wf_ref[:, j * 256:(j + 1) * 256], preferred_element_type=jnp.float32)
    for j in range(5):
        pf_scr[:, j * 256:(j + 1) * 256] = proj(j)
    sgc_ref[0, :, 0:256] = _silu(proj(5))
    sgc_ref[0, :, 256:512] = _silu(proj(6))

    a = pf_scr[:, 0:256]
    halo = jnp.where(s > 0, halo_scr[...], 0.0)
    halo_scr[...] = a[t - POOL_HALO:]
    y_a = _pool_mixer(a, halo, s * t, pool_w_ref[...], pool_scale_ref[...])
    yab_ref[0, :, 0:256] = (y_a * _silu(pf_scr[:, 256:512])).astype(jnp.bfloat16)

    ti = lax.broadcasted_iota(jnp.int32, (SGU_BLOCK, SGU_BLOCK), 0)
    si = lax.broadcasted_iota(jnp.int32, (SGU_BLOCK, SGU_BLOCK), 1)
    chunk_causal = (si // CHUNK) <= (ti // CHUNK)
    w_masked = jnp.where(chunk_causal[None], sgu_w_ref[...], 0.0).astype(jnp.bfloat16)
    y_b = _spatial_gating(pf_scr[:, 512:768], pf_scr[:, 768:1024],
                          sgu_g_ref[...], sgu_b_ref[...], w_masked, sgu_bias_ref[...])
    yab_ref[0, :, 256:512] = (y_b * _silu(pf_scr[:, 1024:1280])).astype(jnp.bfloat16)

    zero = jnp.zeros((ATTN_BLOCK, 128), jnp.bfloat16)
    lane_lo = lax.broadcasted_iota(jnp.int32, (ATTN_BLOCK, 128), 1) < HEAD_DIM
    row_lo = lax.broadcasted_iota(jnp.int32, (ATTN_BLOCK, 128), 0) < HEAD_DIM
    nt = (((1,), (1,)), ((), ()))
    qt_all = lax.dot_general(wqt_ref[...], h, nt, preferred_element_type=jnp.float32)
    qt_all = (qt_all * HEAD_DIM ** -0.5).astype(jnp.bfloat16)
    k_all = jnp.dot(h, wk_ref[...], preferred_element_type=jnp.float32).astype(jnp.bfloat16)
    vt_all = lax.dot_general(wvt_ref[...], h, nt,
                             preferred_element_type=jnp.float32).astype(jnp.bfloat16)
    for p in range(SB_PAIRS):
        cols = slice(p * 128, (p + 1) * 128)
        qt_ref[0, p] = qt_all[cols]
        k = k_all[:, cols]
        vt = vt_all[cols]
        for j in range(t // ATTN_BLOCK):
            kb = k[j * ATTN_BLOCK:(j + 1) * ATTN_BLOCK]
            k_ref[0, p, 2 * j * ATTN_BLOCK:(2 * j + 1) * ATTN_BLOCK, :] = jnp.where(lane_lo, kb, zero)
            k_ref[0, p, (2 * j + 1) * ATTN_BLOCK:(2 * j + 2) * ATTN_BLOCK, :] = jnp.where(lane_lo, zero, kb)
            vb = vt[:, j * ATTN_BLOCK:(j + 1) * ATTN_BLOCK]
            vt_ref[0, p, :, 2 * j * ATTN_BLOCK:(2 * j + 1) * ATTN_BLOCK] = jnp.where(row_lo, vb, zero)
            vt_ref[0, p, :, (2 * j + 1) * ATTN_BLOCK:(2 * j + 2) * ATTN_BLOCK] = jnp.where(row_lo, zero, vb)


def _in_projection(x, mod3, wf, wqt, wk, wvt, pool_w_bd, pool_scale, sgu_g, sgu_b, sgu_w, sgu_bias):
    batch, seq, d = x.shape
    ts = SEQ_TILE
    full = lambda shape: pl.BlockSpec(shape, lambda b, s: (0,) * len(shape))
    small = (pool_w_bd, pool_scale, sgu_g, sgu_b, sgu_w, sgu_bias)
    return pl.pallas_call(
        _inproj_kernel,
        grid=(batch, seq // ts),
        in_specs=[
            pl.BlockSpec((1, ts, d), lambda b, s: (b, s, 0)),
            pl.BlockSpec((1, 1, d), lambda b, s: (b, 0, 0)),
            pl.BlockSpec((1, 1, d), lambda b, s: (b, 0, 1)),
            full(wf.shape), full(wqt.shape), full(wk.shape), full(wvt.shape),
            *[full(w.shape) for w in small],
        ],
        out_specs=[
            pl.BlockSpec((1, ts, POOL_WIDTH + SGU_WIDTH), lambda b, s: (b, s, 0)),
            pl.BlockSpec((1, ts, SB_WIDTH), lambda b, s: (b, s, 0)),
            pl.BlockSpec((1, SB_PAIRS, 128, ts), lambda b, s: (b, 0, 0, s)),
            pl.BlockSpec((1, SB_PAIRS, 2 * ts, 128), lambda b, s: (b, 0, s, 0)),
            pl.BlockSpec((1, SB_PAIRS, 128, 2 * ts), lambda b, s: (b, 0, 0, s)),
        ],
        out_shape=[
            jax.ShapeDtypeStruct((batch, seq, POOL_WIDTH + SGU_WIDTH), jnp.bfloat16),
            jax.ShapeDtypeStruct((batch, seq, SB_WIDTH), jnp.float32),
            jax.ShapeDtypeStruct((batch, SB_PAIRS, 128, seq), jnp.bfloat16),
            jax.ShapeDtypeStruct((batch, SB_PAIRS, 2 * seq, 128), jnp.bfloat16),
            jax.ShapeDtypeStruct((batch, SB_PAIRS, 128, 2 * seq), jnp.bfloat16),
        ],
        scratch_shapes=[pltpu.VMEM((POOL_HALO, POOL_WIDTH), jnp.float32),
                        pltpu.VMEM((ts, 5 * 256), jnp.float32)],
        compiler_params=pltpu.CompilerParams(
            dimension_semantics=("parallel", "arbitrary"),
            vmem_limit_bytes=VMEM_LIMIT_BYTES),
        name="in_projection",
    )(x, mod3, mod3, wf, wqt, wk, wvt, *small)


def _pool_mixer(a, halo, first_row, pool_w_bd, pool_scale):
    t = a.shape[0]
    ext = jnp.concatenate([halo, a], axis=0)
    lane = lax.broadcasted_iota(jnp.int32, ext.shape, 1)
    acc = ext
    total = None
    for g, win in enumerate(POOL_WINDOWS):
        acc = acc + pltpu.roll(acc, win // 2, axis=0)
        in_group = (lane >= g * 64) & (lane < (g + 1) * 64)
        total = jnp.where(in_group, acc, 0.0) if total is None else jnp.where(in_group, acc, total)
    win_sum = total[POOL_HALO:]
    row = lax.broadcasted_iota(jnp.int32, (t, POOL_WIDTH), 0) + first_row
    lane_t = lax.broadcasted_iota(jnp.int32, (t, POOL_WIDTH), 1)
    win_lane = jnp.left_shift(2, lane_t // 64)
    cnt = jnp.minimum(row + 1, win_lane).astype(jnp.float32)
    d = (win_sum / cnt - a).astype(jnp.bfloat16)
    y = jnp.dot(d, pool_w_bd, preferred_element_type=jnp.float32)
    return y * pool_scale


def _spatial_gating(u, v, ln_g, ln_b, w_masked, bias_full):
    t = u.shape[0]
    mu = jnp.mean(v, axis=-1, keepdims=True)
    var = jnp.mean(jnp.square(v - mu), axis=-1, keepdims=True)
    vn = ((v - mu) * lax.rsqrt(var + LN_EPS) * ln_g + ln_b).astype(jnp.bfloat16)
    lane = lax.broadcasted_iota(jnp.int32, (SGU_BLOCK, SGU_WIDTH), 1)
    outs = []
    for n in range(t // SGU_BLOCK):
        vb = vn[n * SGU_BLOCK:(n + 1) * SGU_BLOCK]
        mixed = bias_full
        for h in range(SGU_HEADS):
            r = jnp.dot(w_masked[h], vb, preferred_element_type=jnp.float32)
            in_head = (lane >= h * HEAD_DIM) & (lane < (h + 1) * HEAD_DIM)
            mixed = mixed + jnp.where(in_head, r, 0.0)
        outs.append(mixed)
    return u * jnp.concatenate(outs, axis=0)


def _attention(qt_ref, k_ref, vt_ref, carry_scr, acc_scr, first_group, groups):
    two = 2 * ATTN_BLOCK
    r = lax.broadcasted_iota(jnp.int32, (two, two), 0)
    c = lax.broadcasted_iota(jnp.int32, (two, two), 1)
    same_head = (r >= ATTN_BLOCK) == (c >= ATTN_BLOCK)
    tri2 = jnp.where(same_head & (c >= r), 1.0, 0.0).astype(jnp.bfloat16)
    blocks_per_group = ATTN_GROUP // ATTN_BLOCK
    key = lax.broadcasted_iota(jnp.int32, (two, ATTN_GROUP), 0) & (ATTN_BLOCK - 1)
    query = lax.broadcasted_iota(jnp.int32, (two, ATTN_GROUP), 1)
    causal = {t: key + (blocks_per_group - 1 - t) * ATTN_BLOCK < query for t in range(blocks_per_group)}

    def tile_keys(group, step):
        key_block = (first_group + group + 1) * blocks_per_group - 1 - step
        if isinstance(key_block, int):
            return key_block, key_block * two
        return key_block, pl.multiple_of(jnp.maximum(key_block, 0) * two, two)

    def logits(pair, group, step):
        cols = slice(group * ATTN_GROUP, (group + 1) * ATTN_GROUP)
        _, k0 = tile_keys(group, step)
        z = jnp.dot(k_ref[0, pair, pl.ds(k0, two), :], qt_ref[0, pair, :, cols],
                    preferred_element_type=jnp.float32)
        if isinstance(step, int) and step < blocks_per_group:
            z = jnp.where(causal[step], z, ATTN_MASKED)
        zb = z.astype(jnp.bfloat16)
        return z, jnp.maximum(zb, 0.0) + jnp.log(1.0 + jnp.exp(-jnp.abs(zb)))

    def weights(pair, group, step, z, spb):
        cols = slice(group * ATTN_GROUP, (group + 1) * ATTN_GROUP)
        key_block, _ = tile_keys(group, step)
        suffix = jnp.dot(tri2, spb, preferred_element_type=jnp.float32)
        ws, carries = [], []
        for h in range(2):
            rows = slice(h * ATTN_BLOCK, (h + 1) * ATTN_BLOCK)
            total = suffix[h * ATTN_BLOCK:h * ATTN_BLOCK + 1, :]
            x = z[rows] - suffix[rows]
            if isinstance(step, int) and step == 0:
                carry = total
            else:
                old = carry_scr[pair, h:h + 1, cols]
                if not isinstance(step, int):
                    old = old + jnp.where(key_block >= 0, 0.0, -ATTN_MASKED)
                x = x - old
                carry = old + total
            carry_scr[pair, h:h + 1, cols] = carry
            ws.append(jnp.exp(x).astype(jnp.bfloat16))
            carries.append(carry)
        return jnp.concatenate(ws, axis=0), jnp.minimum(carries[0], carries[1])

    def accumulate(pair, group, step, w):
        cols = slice(group * ATTN_GROUP, (group + 1) * ATTN_GROUP)
        _, k0 = tile_keys(group, step)
        out = jnp.dot(vt_ref[0, pair, :, pl.ds(k0, two)], w,
                      preferred_element_type=jnp.float32)
        if isinstance(step, int) and step == 0:
            acc_scr[pair, :, cols] = out
        else:
            acc_scr[pair, :, cols] += out

    tiles = [(pair, group) for group in range(groups) for pair in range(SB_PAIRS)]

    def visit(work):
        n = len(work)
        zs, ws, carries = {}, {}, {}
        lag_w, lag_a = ATTN_STAGE_LAGS
        for i in range(n + lag_a):
            if i < n:
                zs[i] = logits(*work[i])
            if 0 <= i - lag_w < n:
                ws[i - lag_w], carries[i - lag_w] = weights(*work[i - lag_w], *zs.pop(i - lag_w))
            if 0 <= i - lag_a < n:
                accumulate(*work[i - lag_a], ws.pop(i - lag_a))
        return [(work[i], carries[i]) for i in range(n - len(tiles), n)]

    def more_to_do(last):
        smallest = None
        for (_, group, step), carry in last:
            key_block, _ = tile_keys(group, step)
            m = jnp.where(key_block > 0, carry, ATTN_SKIP)
            smallest = m if smallest is None else jnp.minimum(smallest, m)
        return jnp.min(smallest) < ATTN_SKIP

    if isinstance(first_group, int):
        assert first_group == 0
        most = groups * blocks_per_group
        visit([(pair, group, step) for step in range(most) for pair, group in tiles
               if step < (group + 1) * blocks_per_group])
        return

    def body(state):
        step, _ = state
        steps = [step + i for i in range(ATTN_STEPS_PER_VISIT)]
        return step + ATTN_STEPS_PER_VISIT, more_to_do(
            visit([(pair, group, st) for st in steps for pair, group in tiles]))

    first_steps = min(ATTN_FIRST_VISIT_STEPS, (groups + 1) * blocks_per_group)
    live = more_to_do(visit([(pair, group, step) for step in range(first_steps) for pair, group in tiles]))
    lax.while_loop(lambda state: state[1], body, (jnp.int32(first_steps), live))


def _mixer_kernel(yab_ref, sgc_ref, qt_ref, k_ref, vt_ref, x_ref, gate_ref,
                  wout_ref, lng_ref, lnb_ref, o_ref, y_scr, att_scr, carry_scr, *, alpha):
    s = pl.program_id(1)
    t = x_ref.shape[1]

    groups_per_tile = t // ATTN_GROUP

    @pl.when(s == 0)
    def _():
        _attention(qt_ref, k_ref, vt_ref, carry_scr, att_scr, 0, groups_per_tile)

    @pl.when(s > 0)
    def _():
        _attention(qt_ref, k_ref, vt_ref, carry_scr, att_scr, s * groups_per_tile, groups_per_tile)

    y_scr[:, 0:POOL_WIDTH + SGU_WIDTH] = yab_ref[0]
    for p in range(SB_PAIRS):
        cols = slice(p * 128, (p + 1) * 128)
        y_scr[:, SB_WIDTH + p * 128:SB_WIDTH + (p + 1) * 128] = (
            att_scr[p].T * sgc_ref[0, :, cols]).astype(jnp.bfloat16)

    for i in range(t // OUT_ROWS):
        rows = slice(i * OUT_ROWS, (i + 1) * OUT_ROWS)
        y = jnp.dot(y_scr[rows, :], wout_ref[...], preferred_element_type=jnp.float32)
        r = alpha * x_ref[0, rows, :] + gate_ref[0] * y
        mu = jnp.mean(r, axis=-1, keepdims=True)
        var = jnp.mean(jnp.square(r - mu), axis=-1, keepdims=True)
        o_ref[0, rows, :] = (r - mu) * lax.rsqrt(var + LN_EPS) * lng_ref[...] + lnb_ref[...]


def _mixers(yab, sgc, qt, k, vt, x, mod3, wout, ln_g, ln_b, alpha):
    batch, seq, d = x.shape
    t = SEQ_TILE
    full = lambda shape: pl.BlockSpec(shape, lambda b, s: (0,) * len(shape))
    return pl.pallas_call(
        functools.partial(_mixer_kernel, alpha=alpha),
        grid=(batch, seq // t),
        in_specs=[
            pl.BlockSpec((1, t, POOL_WIDTH + SGU_WIDTH), lambda b, s: (b, s, 0)),
            pl.BlockSpec((1, t, SB_WIDTH), lambda b, s: (b, s, 0)),
            pl.BlockSpec((1, SB_PAIRS, 128, t), lambda b, s: (b, 0, 0, s)),
            pl.BlockSpec((1, SB_PAIRS, 2 * seq, 128), lambda b, s: (b, 0, 0, 0)),
            pl.BlockSpec((1, SB_PAIRS, 128, 2 * seq), lambda b, s: (b, 0, 0, 0)),
            pl.BlockSpec((1, t, d), lambda b, s: (b, s, 0)),
            pl.BlockSpec((1, 1, d), lambda b, s: (b, 0, 2)),
            full(wout.shape), full(ln_g.shape), full(ln_b.shape),
        ],
        out_specs=pl.BlockSpec((1, t, d), lambda b, s: (b, s, 0)),
        out_shape=jax.ShapeDtypeStruct((batch, seq, d), jnp.float32),
        scratch_shapes=[
            pltpu.VMEM((t, d), jnp.bfloat16),
            pltpu.VMEM((SB_PAIRS, 128, t), jnp.float32),
            pltpu.VMEM((SB_PAIRS, 2, t), jnp.float32),
        ],
        compiler_params=pltpu.CompilerParams(
            dimension_semantics=("parallel", "arbitrary"),
            vmem_limit_bytes=VMEM_LIMIT_BYTES),
        name="mixers_outproj",
    )(yab, sgc, qt, k, vt, x, mod3, wout, ln_g, ln_b)


def _block_diag(w):
    g, n, _ = w.shape
    out = jnp.zeros((g * n, g * n), w.dtype)
    for i in range(g):
        out = out.at[i * n:(i + 1) * n, i * n:(i + 1) * n].set(w[i])
    return out


def kernel(x, c, w_in, pool_w, pool_scale, sgu_ln_g, sgu_ln_b, sgu_w, sgu_b, w_out, ada_w, ada_b,
           ln_g, ln_b):
    depth = w_in.shape[0]
    batch = x.shape[0]
    alpha = (2 * depth) ** 0.25
    mod = _modulation(c, ada_w, ada_b)
    bf = jnp.bfloat16
    for l in range(depth):
        w = w_in[l]
        wf = jnp.concatenate([w[:, 0:1280], w[:, 2816:3328]], axis=1).astype(bf)
        wqt = w[:, 1280:1792].T.astype(bf)
        wk = w[:, 1792:2304].astype(bf)
        wvt = w[:, 2304:2816].T.astype(bf)
        mod3 = mod[l].reshape(batch, 1, 3 * D_MODEL)
        yab, sgc, qt, k, vt = _in_projection(
            x, mod3, wf, wqt, wk, wvt,
            _block_diag(pool_w[l]).astype(bf), pool_scale[l].reshape(1, POOL_WIDTH),
            sgu_ln_g[l].reshape(1, SGU_WIDTH), sgu_ln_b[l].reshape(1, SGU_WIDTH),
            sgu_w[l], jnp.repeat(sgu_b[l].T, HEAD_DIM, axis=1))
        x = _mixers(yab, sgc, qt, k, vt, x, mod3, w_out[l].astype(bf),
                    ln_g[l].reshape(1, D_MODEL), ln_b[l].reshape(1, D_MODEL), alpha)
    return x
```

```python
import functools

import jax
import jax.numpy as jnp
from jax import lax
from jax.experimental import pallas as pl
from jax.experimental.pallas import tpu as pltpu

D_MODEL = 1024
HEAD_DIM = 64
POOL_WINDOWS = (2, 4, 8, 16)
POOL_WIDTH = 256
POOL_HALO = 16
SGU_WIDTH = 256
SGU_HEADS = 4
SGU_BLOCK = 128
CHUNK = 64
SB_WIDTH = 512
SB_PAIRS = SB_WIDTH // (2 * HEAD_DIM)
ATTN_BLOCK = 128
ATTN_GROUP = 256
ATTN_STAGE_LAGS = (3, 6)
ATTN_STEPS_PER_VISIT = 2
ATTN_FIRST_VISIT_STEPS = 6
ATTN_SKIP = 104.0
ATTN_MASKED = -1e30
LN_EPS = 1e-5

SEQ_TILE = 512
OUT_ROWS = 256
VMEM_LIMIT_BYTES = 52 * 1024 * 1024


def _silu(g):
    return g / (1.0 + jnp.exp(-g))


def _mod_kernel(c_ref, w_ref, b_ref, o_ref):
    c = c_ref[...]
    sc = _silu(c)
    o_ref[0] = jnp.dot(sc, w_ref[0], preferred_element_type=jnp.float32,
                       precision=lax.Precision.HIGHEST) + b_ref[0]


def _modulation(c, ada_w, ada_b):
    depth, d, n = ada_w.shape
    batch = c.shape[0]
    tn = 512
    return pl.pallas_call(
        _mod_kernel,
        grid=(depth, n // tn),
        in_specs=[
            pl.BlockSpec((batch, d), lambda l, j: (0, 0)),
            pl.BlockSpec((1, d, tn), lambda l, j: (l, 0, j)),
            pl.BlockSpec((1, 1, tn), lambda l, j: (l, 0, j)),
        ],
        out_specs=pl.BlockSpec((1, batch, tn), lambda l, j: (l, 0, j)),
        out_shape=jax.ShapeDtypeStruct((depth, batch, n), jnp.float32),
        name="adaln_mod",
    )(c, ada_w, ada_b.reshape(depth, 1, n))


def _inproj_kernel(x_ref, shift_ref, scale_ref, wf_ref, wqt_ref, wk_ref, wvt_ref,
                   pool_w_ref, pool_scale_ref, sgu_g_ref, sgu_b_ref, sgu_w_ref, sgu_bias_ref,
                   yab_ref, gc_ref, qt_ref, k_ref, vt_ref, halo_scr, pf_scr):
    s = pl.program_id(1)
    h = (x_ref[0] * (1.0 + scale_ref[0]) + shift_ref[0]).astype(jnp.bfloat16)
    t = h.shape[0]
    proj = lambda j: jnp.dot(h, wf_ref[:, j * 256:(j + 1) * 256], preferred_element_type=jnp.float32)
    for j in range(5):
        pf_scr[:, j * 256:(j + 1) * 256] = proj(j)
    gc_ref[0, :, 0:256] = proj(5)
    gc_ref[0, :, 256:512] = proj(6)

    a = pf_scr[:, 0:256]
    halo = jnp.where(s > 0, halo_scr[...], 0.0)
    halo_scr[...] = a[t - POOL_HALO:]
    y_a = _pool_mixer(a, halo, s * t, pool_w_ref[...], pool_scale_ref[...])
    yab_ref[0, :, 0:256] = (y_a * _silu(pf_scr[:, 256:512])).astype(jnp.bfloat16)

    ti = lax.broadcasted_iota(jnp.int32, (SGU_BLOCK, SGU_BLOCK), 0)
    si = lax.broadcasted_iota(jnp.int32, (SGU_BLOCK, SGU_BLOCK), 1)
    chunk_causal = (si // CHUNK) <= (ti // CHUNK)
    w_masked = jnp.where(chunk_causal[None], sgu_w_ref[...], 0.0).astype(jnp.bfloat16)
    y_b = _spatial_gating(pf_scr[:, 512:768], pf_scr[:, 768:1024],
                          sgu_g_ref[...], sgu_b_ref[...], w_masked, sgu_bias_ref[...])
    yab_ref[0, :, 256:512] = (y_b * _silu(pf_scr[:, 1024:1280])).astype(jnp.bfloat16)

    zero = jnp.zeros((ATTN_BLOCK, 128), jnp.bfloat16)
    lane_lo = lax.broadcasted_iota(jnp.int32, (ATTN_BLOCK, 128), 1) < HEAD_DIM
    row_lo = lax.broadcasted_iota(jnp.int32, (ATTN_BLOCK, 128), 0) < HEAD_DIM
    nt = (((1,), (1,)), ((), ()))
    qt_all = lax.dot_general(wqt_ref[...], h, nt, preferred_element_type=jnp.float32)
    qt_all = (qt_all * HEAD_DIM ** -0.5).astype(jnp.bfloat16)
    k_all = jnp.dot(h, wk_ref[...], preferred_element_type=jnp.float32).astype(jnp.bfloat16)
    vt_all = lax.dot_general(wvt_ref[...], h, nt,
                             preferred_element_type=jnp.float32).astype(jnp.bfloat16)
    for p in range(SB_PAIRS):
        cols = slice(p * 128, (p + 1) * 128)
        qt_ref[0, p] = qt_all[cols]
        k = k_all[:, cols]
        vt = vt_all[cols]
        for j in range(t // ATTN_BLOCK):
            kb = k[j * ATTN_BLOCK:(j + 1) * ATTN_BLOCK]
            k_ref[0, p, 2 * j * ATTN_BLOCK:(2 * j + 1) * ATTN_BLOCK, :] = jnp.where(lane_lo, kb, zero)
            k_ref[0, p, (2 * j + 1) * ATTN_BLOCK:(2 * j + 2) * ATTN_BLOCK, :] = jnp.where(lane_lo, zero, kb)
            vb = vt[:, j * ATTN_BLOCK:(j + 1) * ATTN_BLOCK]
            vt_ref[0, p, :, 2 * j * ATTN_BLOCK:(2 * j + 1) * ATTN_BLOCK] = jnp.where(row_lo, vb, zero)
            vt_ref[0, p, :, (2 * j + 1) * ATTN_BLOCK:(2 * j + 2) * ATTN_BLOCK] = jnp.where(row_lo, zero, vb)


def _in_projection(x, mod3, wf, wqt, wk, wvt, pool_w_bd, pool_scale, sgu_g, sgu_b, sgu_w, sgu_bias):
    batch, seq, d = x.shape
    ts = SEQ_TILE
    full = lambda shape: pl.BlockSpec(shape, lambda b, s: (0,) * len(shape))
    small = (pool_w_bd, pool_scale, sgu_g, sgu_b, sgu_w, sgu_bias)
    return pl.pallas_call(
        _inproj_kernel,
        grid=(batch, seq // ts),
        in_specs=[
            pl.BlockSpec((1, ts, d), lambda b, s: (b, s, 0)),
            pl.BlockSpec((1, 1, d), lambda b, s: (b, 0, 0)),
            pl.BlockSpec((1, 1, d), lambda b, s: (b, 0, 1)),
            full(wf.shape), full(wqt.shape), full(wk.shape), full(wvt.shape),
            *[full(w.shape) for w in small],
        ],
        out_specs=[
            pl.BlockSpec((1, ts, POOL_WIDTH + SGU_WIDTH), lambda b, s: (b, s, 0)),
            pl.BlockSpec((1, ts, SB_WIDTH), lambda b, s: (b, s, 0)),
            pl.BlockSpec((1, SB_PAIRS, 128, ts), lambda b, s: (b, 0, 0, s)),
            pl.BlockSpec((1, SB_PAIRS, 2 * ts, 128), lambda b, s: (b, 0, s, 0)),
            pl.BlockSpec((1, SB_PAIRS, 128, 2 * ts), lambda b, s: (b, 0, 0, s)),
        ],
        out_shape=[
            jax.ShapeDtypeStruct((batch, seq, POOL_WIDTH + SGU_WIDTH), jnp.bfloat16),
            jax.ShapeDtypeStruct((batch, seq, SB_WIDTH), jnp.float32),
            jax.ShapeDtypeStruct((batch, SB_PAIRS, 128, seq), jnp.bfloat16),
            jax.ShapeDtypeStruct((batch, SB_PAIRS, 2 * seq, 128), jnp.bfloat16),
            jax.ShapeDtypeStruct((batch, SB_PAIRS, 128, 2 * seq), jnp.bfloat16),
        ],
        scratch_shapes=[pltpu.VMEM((POOL_HALO, POOL_WIDTH), jnp.float32),
                        pltpu.VMEM((ts, 5 * 256), jnp.float32)],
        compiler_params=pltpu.CompilerParams(
            dimension_semantics=("parallel", "arbitrary"),
            vmem_limit_bytes=VMEM_LIMIT_BYTES),
        name="in_projection",
    )(x, mod3, mod3, wf, wqt, wk, wvt, *small)


def _pool_mixer(a, halo, first_row, pool_w_bd, pool_scale):
    t = a.shape[0]
    ext = jnp.concatenate([halo, a], axis=0)
    lane = lax.broadcasted_iota(jnp.int32, ext.shape, 1)
    acc = ext
    total = None
    for g, win in enumerate(POOL_WINDOWS):
        acc = acc + pltpu.roll(acc, win // 2, axis=0)
        in_group = (lane >= g * 64) & (lane < (g + 1) * 64)
        total = jnp.where(in_group, acc, 0.0) if total is None else jnp.where(in_group, acc, total)
    win_sum = total[POOL_HALO:]
    row = lax.broadcasted_iota(jnp.int32, (t, POOL_WIDTH), 0) + first_row
    lane_t = lax.broadcasted_iota(jnp.int32, (t, POOL_WIDTH), 1)
    win_lane = jnp.left_shift(2, lane_t // 64)
    cnt = jnp.minimum(row + 1, win_lane).astype(jnp.float32)
    d = (win_sum / cnt - a).astype(jnp.bfloat16)
    y = jnp.dot(d, pool_w_bd, preferred_element_type=jnp.float32)
    return y * pool_scale


def _spatial_gating(u, v, ln_g, ln_b, w_masked, bias_full):
    t = u.shape[0]
    mu = jnp.mean(v, axis=-1, keepdims=True)
    var = jnp.mean(jnp.square(v - mu), axis=-1, keepdims=True)
    vn = ((v - mu) * lax.rsqrt(var + LN_EPS) * ln_g + ln_b).astype(jnp.bfloat16)
    lane = lax.broadcasted_iota(jnp.int32, (SGU_BLOCK, SGU_WIDTH), 1)
    outs = []
    for n in range(t // SGU_BLOCK):
        vb = vn[n * SGU_BLOCK:(n + 1) * SGU_BLOCK]
        mixed = bias_full
        for h in range(SGU_HEADS):
            r = jnp.dot(w_masked[h], vb, preferred_element_type=jnp.float32)
            in_head = (lane >= h * HEAD_DIM) & (lane < (h + 1) * HEAD_DIM)
            mixed = mixed + jnp.where(in_head, r, 0.0)
        outs.append(mixed)
    return u * jnp.concatenate(outs, axis=0)


def _attention(qt_ref, k_ref, vt_ref, carry_scr, acc_scr, first_group, groups):
    two = 2 * ATTN_BLOCK
    r = lax.broadcasted_iota(jnp.int32, (two, two), 0)
    c = lax.broadcasted_iota(jnp.int32, (two, two), 1)
    same_head = (r >= ATTN_BLOCK) == (c >= ATTN_BLOCK)
    tri2 = jnp.where(same_head & (c >= r), 1.0, 0.0).astype(jnp.bfloat16)
    blocks_per_group = ATTN_GROUP // ATTN_BLOCK
    key = lax.broadcasted_iota(jnp.int32, (two, ATTN_GROUP), 0) & (ATTN_BLOCK - 1)
    query = lax.broadcasted_iota(jnp.int32, (two, ATTN_GROUP), 1)
    causal = {t: key + (blocks_per_group - 1 - t) * ATTN_BLOCK < query for t in range(blocks_per_group)}

    def tile_keys(group, step):
        key_block = (first_group + group + 1) * blocks_per_group - 1 - step
        if isinstance(key_block, int):
            return key_block, key_block * two
        return key_block, pl.multiple_of(jnp.maximum(key_block, 0) * two, two)

    def logits(pair, group, step):
        cols = slice(group * ATTN_GROUP, (group + 1) * ATTN_GROUP)
        _, k0 = tile_keys(group, step)
        z = jnp.dot(k_ref[0, pair, pl.ds(k0, two), :], qt_ref[0, pair, :, cols],
                    preferred_element_type=jnp.float32)
        if isinstance(step, int) and step < blocks_per_group:
            z = jnp.where(causal[step], z, ATTN_MASKED)
        zb = z.astype(jnp.bfloat16)
        return z, jnp.maximum(zb, 0.0) + jnp.log(1.0 + jnp.exp(-jnp.abs(zb)))

    def weights(pair, group, step, z, spb):
        cols = slice(group * ATTN_GROUP, (group + 1) * ATTN_GROUP)
        key_block, _ = tile_keys(group, step)
        suffix = jnp.dot(tri2, spb, preferred_element_type=jnp.float32)
        ws, carries = [], []
        for h in range(2):
            rows = slice(h * ATTN_BLOCK, (h + 1) * ATTN_BLOCK)
            total = suffix[h * ATTN_BLOCK:h * ATTN_BLOCK + 1, :]
            x = z[rows] - suffix[rows]
            if isinstance(step, int) and step == 0:
                carry = total
            else:
                old = carry_scr[pair, h:h + 1, cols]
                if not isinstance(step, int):
                    old = old + jnp.where(key_block >= 0, 0.0, -ATTN_MASKED)
                x = x - old
                carry = old + total
            carry_scr[pair, h:h + 1, cols] = carry
            ws.append(jnp.exp(x).astype(jnp.bfloat16))
            carries.append(carry)
        return jnp.concatenate(ws, axis=0), jnp.minimum(carries[0], carries[1])

    def accumulate(pair, group, step, w):
        cols = slice(group * ATTN_GROUP, (group + 1) * ATTN_GROUP)
        _, k0 = tile_keys(group, step)
        out = jnp.dot(vt_ref[0, pair, :, pl.ds(k0, two)], w,
                      preferred_element_type=jnp.float32)
        if isinstance(step, int) and step == 0:
            acc_scr[pair, :, cols] = out
        else:
            acc_scr[pair, :, cols] += out

    tiles = [(pair, group) for group in range(groups) for pair in range(SB_PAIRS)]

    def visit(work):
        n = len(work)
        zs, ws, carries = {}, {}, {}
        lag_w, lag_a = ATTN_STAGE_LAGS
        for i in range(n + lag_a):
            if i < n:
                zs[i] = logits(*work[i])
            if 0 <= i - lag_w < n:
                ws[i - lag_w], carries[i - lag_w] = weights(*work[i - lag_w], *zs.pop(i - lag_w))
            if 0 <= i - lag_a < n:
                accumulate(*work[i - lag_a], ws.pop(i - lag_a))
        return [(work[i], carries[i]) for i in range(n - len(tiles), n)]

    def more_to_do(last):
        smallest = None
        for (_, group, step), carry in last:
            key_block, _ = tile_keys(group, step)
            m = jnp.where(key_block > 0, carry, ATTN_SKIP)
            smallest = m if smallest is None else jnp.minimum(smallest, m)
        return jnp.min(smallest) < ATTN_SKIP

    if isinstance(first_group, int):
        assert first_group == 0
        most = groups * blocks_per_group
        visit([(pair, group, step) for step in range(most) for pair, group in tiles
               if step < (group + 1) * blocks_per_group])
        return

    def body(state):
        step, _ = state
        steps = [step + i for i in range(ATTN_STEPS_PER_VISIT)]
        return step + ATTN_STEPS_PER_VISIT, more_to_do(
            visit([(pair, group, st) for st in steps for pair, group in tiles]))

    first_steps = min(ATTN_FIRST_VISIT_STEPS, (groups + 1) * blocks_per_group)
    live = more_to_do(visit([(pair, group, step) for step in range(first_steps) for pair, group in tiles]))
    lax.while_loop(lambda state: state[1], body, (jnp.int32(first_steps), live))


def _mixer_kernel(yab_ref, gc_ref, qt_ref, k_ref, vt_ref, x_ref, gate_ref,
                  wout_ref, lng_ref, lnb_ref, o_ref, y_scr, att_scr, carry_scr, *, alpha):
    s = pl.program_id(1)
    t = x_ref.shape[1]

    groups_per_tile = t // ATTN_GROUP

    @pl.when(s == 0)
    def _():
        _attention(qt_ref, k_ref, vt_ref, carry_scr, att_scr, 0, groups_per_tile)

    @pl.when(s > 0)
    def _():
        _attention(qt_ref, k_ref, vt_ref, carry_scr, att_scr, s * groups_per_tile, groups_per_tile)

    y_scr[:, 0:POOL_WIDTH + SGU_WIDTH] = yab_ref[0]
    for p in range(SB_PAIRS):
        cols = slice(p * 128, (p + 1) * 128)
        y_scr[:, SB_WIDTH + p * 128:SB_WIDTH + (p + 1) * 128] = (
            att_scr[p].T * _silu(gc_ref[0, :, cols])).astype(jnp.bfloat16)

    for i in range(t // OUT_ROWS):
        rows = slice(i * OUT_ROWS, (i + 1) * OUT_ROWS)
        y = jnp.dot(y_scr[rows, :], wout_ref[...], preferred_element_type=jnp.float32)
        r = alpha * x_ref[0, rows, :] + gate_ref[0] * y
        mu = jnp.mean(r, axis=-1, keepdims=True)
        var = jnp.mean(jnp.square(r - mu), axis=-1, keepdims=True)
        o_ref[0, rows, :] = (r - mu) * lax.rsqrt(var + LN_EPS) * lng_ref[...] + lnb_ref[...]


def _mixers(yab, gc, qt, k, vt, x, mod3, wout, ln_g, ln_b, alpha):
    batch, seq, d = x.shape
    t = SEQ_TILE
    full = lambda shape: pl.BlockSpec(shape, lambda b, s: (0,) * len(shape))
    return pl.pallas_call(
        functools.partial(_mixer_kernel, alpha=alpha),
        grid=(batch, seq // t),
        in_specs=[
            pl.BlockSpec((1, t, POOL_WIDTH + SGU_WIDTH), lambda b, s: (b, s, 0)),
            pl.BlockSpec((1, t, SB_WIDTH), lambda b, s: (b, s, 0)),
            pl.BlockSpec((1, SB_PAIRS, 128, t), lambda b, s: (b, 0, 0, s)),
            pl.BlockSpec((1, SB_PAIRS, 2 * seq, 128), lambda b, s: (b, 0, 0, 0)),
            pl.BlockSpec((1, SB_PAIRS, 128, 2 * seq), lambda b, s: (b, 0, 0, 0)),
            pl.BlockSpec((1, t, d), lambda b, s: (b, s, 0)),
            pl.BlockSpec((1, 1, d), lambda b, s: (b, 0, 2)),
            full(wout.shape), full(ln_g.shape), full(ln_b.shape),
        ],
        out_specs=pl.BlockSpec((1, t, d), lambda b, s: (b, s, 0)),
        out_shape=jax.ShapeDtypeStruct((batch, seq, d), jnp.float32),
        scratch_shapes=[
            pltpu.VMEM((t, d), jnp.bfloat16),
            pltpu.VMEM((SB_PAIRS, 128, t), jnp.float32),
            pltpu.VMEM((SB_PAIRS, 2, t), jnp.float32),
        ],
        compiler_params=pltpu.CompilerParams(
            dimension_semantics=("parallel", "arbitrary"),
            vmem_limit_bytes=VMEM_LIMIT_BYTES),
        name="mixers_outproj",
    )(yab, gc, qt, k, vt, x, mod3, wout, ln_g, ln_b)


def _block_diag(w):
    g, n, _ = w.shape
    out = jnp.zeros((g * n, g * n), w.dtype)
    for i in range(g):
        out = out.at[i * n:(i + 1) * n, i * n:(i + 1) * n].set(w[i])
    return out


def kernel(x, c, w_in, pool_w, pool_scale, sgu_ln_g, sgu_ln_b, sgu_w, sgu_b, w_out, ada_w, ada_b,
           ln_g, ln_b):
    depth = w_in.shape[0]
    batch = x.shape[0]
    alpha = (2 * depth) ** 0.25
    mod = _modulation(c, ada_w, ada_b)
    bf = jnp.bfloat16
    for l in range(depth):
        w = w_in[l]
        wf = jnp.concatenate([w[:, 0:1280], w[:, 2816:3328]], axis=1).astype(bf)
        wqt = w[:, 1280:1792].T.astype(bf)
        wk = w[:, 1792:2304].astype(bf)
        wvt = w[:, 2304:2816].T.astype(bf)
        mod3 = mod[l].reshape(batch, 1, 3 * D_MODEL)
        yab, gc, qt, k, vt = _in_projection(
            x, mod3, wf, wqt, wk, wvt,
            _block_diag(pool_w[l]).astype(bf), pool_scale[l].reshape(1, POOL_WIDTH),
            sgu_ln_g[l].reshape(1, SGU_WIDTH), sgu_ln_b[l].reshape(1, SGU_WIDTH),
            sgu_w[l], jnp.repeat(sgu_b[l].T, HEAD_DIM, axis=1))
        x = _mixers(yab, gc, qt, k, vt, x, mod3, w_out[l].astype(bf),
                    ln_g[l].reshape(1, D_MODEL), ln_b[l].reshape(1, D_MODEL), alpha)
    return x
```

```python
import functools

import jax
import jax.numpy as jnp
from jax import lax
from jax.experimental import pallas as pl
from jax.experimental.pallas import tpu as pltpu

D_MODEL = 1024
HEAD_DIM = 64
POOL_WINDOWS = (2, 4, 8, 16)
POOL_WIDTH = 256
POOL_HALO = 16
SGU_WIDTH = 256
SGU_HEADS = 4
SGU_BLOCK = 128
CHUNK = 64
SB_WIDTH = 512
SB_PAIRS = SB_WIDTH // (2 * HEAD_DIM)
ATTN_BLOCK = 128
ATTN_GROUP = 256
ATTN_STAGE_LAGS = (3, 6)
ATTN_STEPS_PER_VISIT = 2
ATTN_FIRST_VISIT_STEPS = 6
ATTN_SKIP = 104.0
ATTN_MASKED = -1e30
LN_EPS = 1e-5

SEQ_TILE = 512
PROJ_TILE = 1024
OUT_ROWS = 256
VMEM_LIMIT_BYTES = 52 * 1024 * 1024


def _silu(g):
    return g / (1.0 + jnp.exp(-g))


def _mod_kernel(c_ref, w_ref, b_ref, o_ref):
    c = c_ref[...]
    sc = _silu(c)
    o_ref[0] = jnp.dot(sc, w_ref[0], preferred_element_type=jnp.float32,
                       precision=lax.Precision.HIGHEST) + b_ref[0]


def _modulation(c, ada_w, ada_b):
    depth, d, n = ada_w.shape
    batch = c.shape[0]
    tn = 512
    return pl.pallas_call(
        _mod_kernel,
        grid=(depth, n // tn),
        in_specs=[
            pl.BlockSpec((batch, d), lambda l, j: (0, 0)),
            pl.BlockSpec((1, d, tn), lambda l, j: (l, 0, j)),
            pl.BlockSpec((1, 1, tn), lambda l, j: (l, 0, j)),
        ],
        out_specs=pl.BlockSpec((1, batch, tn), lambda l, j: (l, 0, j)),
        out_shape=jax.ShapeDtypeStruct((depth, batch, n), jnp.float32),
        name="adaln_mod",
    )(c, ada_w, ada_b.reshape(depth, 1, n))


def _inproj_kernel(x_ref, shift_ref, scale_ref, wf_ref, wqt_ref, wk_ref, wvt_ref,
                   pool_w_ref, pool_scale_ref, sgu_g_ref, sgu_b_ref, sgu_w_ref, sgu_bias_ref,
                   yab_ref, gc_ref, qt_ref, k_ref, vt_ref, halo_scr, pf_scr):
    s = pl.program_id(1)
    h = (x_ref[0] * (1.0 + scale_ref[0]) + shift_ref[0]).astype(jnp.bfloat16)
    t = h.shape[0]
    proj = lambda j: jnp.dot(h, wf_ref[:, j * 256:(j + 1) * 256], preferred_element_type=jnp.float32)
    for j in range(5):
        pf_scr[:, j * 256:(j + 1) * 256] = proj(j)
    gc_ref[0, :, 0:256] = proj(5)
    gc_ref[0, :, 256:512] = proj(6)

    a = pf_scr[:, 0:256]
    halo = jnp.where(s > 0, halo_scr[...], 0.0)
    halo_scr[...] = a[t - POOL_HALO:]
    y_a = _pool_mixer(a, halo, s * t, pool_w_ref[...], pool_scale_ref[...])
    yab_ref[0, :, 0:256] = (y_a * _silu(pf_scr[:, 256:512])).astype(jnp.bfloat16)

    ti = lax.broadcasted_iota(jnp.int32, (SGU_BLOCK, SGU_BLOCK), 0)
    si = lax.broadcasted_iota(jnp.int32, (SGU_BLOCK, SGU_BLOCK), 1)
    chunk_causal = (si // CHUNK) <= (ti // CHUNK)
    w_masked = jnp.where(chunk_causal[None], sgu_w_ref[...], 0.0).astype(jnp.bfloat16)
    y_b = _spatial_gating(pf_scr[:, 512:768], pf_scr[:, 768:1024],
                          sgu_g_ref[...], sgu_b_ref[...], w_masked, sgu_bias_ref[...])
    yab_ref[0, :, 256:512] = (y_b * _silu(pf_scr[:, 1024:1280])).astype(jnp.bfloat16)

    zero = jnp.zeros((ATTN_BLOCK, 128), jnp.bfloat16)
    lane_lo = lax.broadcasted_iota(jnp.int32, (ATTN_BLOCK, 128), 1) < HEAD_DIM
    row_lo = lax.broadcasted_iota(jnp.int32, (ATTN_BLOCK, 128), 0) < HEAD_DIM
    nt = (((1,), (1,)), ((), ()))
    qt_all = lax.dot_general(wqt_ref[...], h, nt, preferred_element_type=jnp.float32)
    qt_all = (qt_all * HEAD_DIM ** -0.5).astype(jnp.bfloat16)
    k_all = jnp.dot(h, wk_ref[...], preferred_element_type=jnp.float32).astype(jnp.bfloat16)
    vt_all = lax.dot_general(wvt_ref[...], h, nt,
                             preferred_element_type=jnp.float32).astype(jnp.bfloat16)
    for p in range(SB_PAIRS):
        cols = slice(p * 128, (p + 1) * 128)
        qt_ref[0, p] = qt_all[cols]
        k = k_all[:, cols]
        vt = vt_all[cols]
        for j in range(t // ATTN_BLOCK):
            kb = k[j * ATTN_BLOCK:(j + 1) * ATTN_BLOCK]
            k_ref[0, p, 2 * j * ATTN_BLOCK:(2 * j + 1) * ATTN_BLOCK, :] = jnp.where(lane_lo, kb, zero)
            k_ref[0, p, (2 * j + 1) * ATTN_BLOCK:(2 * j + 2) * ATTN_BLOCK, :] = jnp.where(lane_lo, zero, kb)
            vb = vt[:, j * ATTN_BLOCK:(j + 1) * ATTN_BLOCK]
            vt_ref[0, p, :, 2 * j * ATTN_BLOCK:(2 * j + 1) * ATTN_BLOCK] = jnp.where(row_lo, vb, zero)
            vt_ref[0, p, :, (2 * j + 1) * ATTN_BLOCK:(2 * j + 2) * ATTN_BLOCK] = jnp.where(row_lo, zero, vb)


def _in_projection(x, mod3, wf, wqt, wk, wvt, pool_w_bd, pool_scale, sgu_g, sgu_b, sgu_w, sgu_bias):
    batch, seq, d = x.shape
    ts = PROJ_TILE
    full = lambda shape: pl.BlockSpec(shape, lambda b, s: (0,) * len(shape))
    small = (pool_w_bd, pool_scale, sgu_g, sgu_b, sgu_w, sgu_bias)
    return pl.pallas_call(
        _inproj_kernel,
        grid=(batch, seq // ts),
        in_specs=[
            pl.BlockSpec((1, ts, d), lambda b, s: (b, s, 0)),
            pl.BlockSpec((1, 1, d), lambda b, s: (b, 0, 0)),
            pl.BlockSpec((1, 1, d), lambda b, s: (b, 0, 1)),
            full(wf.shape), full(wqt.shape), full(wk.shape), full(wvt.shape),
            *[full(w.shape) for w in small],
        ],
        out_specs=[
            pl.BlockSpec((1, ts, POOL_WIDTH + SGU_WIDTH), lambda b, s: (b, s, 0)),
            pl.BlockSpec((1, ts, SB_WIDTH), lambda b, s: (b, s, 0)),
            pl.BlockSpec((1, SB_PAIRS, 128, ts), lambda b, s: (b, 0, 0, s)),
            pl.BlockSpec((1, SB_PAIRS, 2 * ts, 128), lambda b, s: (b, 0, s, 0)),
            pl.BlockSpec((1, SB_PAIRS, 128, 2 * ts), lambda b, s: (b, 0, 0, s)),
        ],
        out_shape=[
            jax.ShapeDtypeStruct((batch, seq, POOL_WIDTH + SGU_WIDTH), jnp.bfloat16),
            jax.ShapeDtypeStruct((batch, seq, SB_WIDTH), jnp.float32),
            jax.ShapeDtypeStruct((batch, SB_PAIRS, 128, seq), jnp.bfloat16),
            jax.ShapeDtypeStruct((batch, SB_PAIRS, 2 * seq, 128), jnp.bfloat16),
            jax.ShapeDtypeStruct((batch, SB_PAIRS, 128, 2 * seq), jnp.bfloat16),
        ],
        scratch_shapes=[pltpu.VMEM((POOL_HALO, POOL_WIDTH), jnp.float32),
                        pltpu.VMEM((ts, 5 * 256), jnp.float32)],
        compiler_params=pltpu.CompilerParams(
            dimension_semantics=("parallel", "arbitrary"),
            vmem_limit_bytes=VMEM_LIMIT_BYTES),
        name="in_projection",
    )(x, mod3, mod3, wf, wqt, wk, wvt, *small)


def _pool_mixer(a, halo, first_row, pool_w_bd, pool_scale):
    t = a.shape[0]
    ext = jnp.concatenate([halo, a], axis=0)
    lane = lax.broadcasted_iota(jnp.int32, ext.shape, 1)
    acc = ext
    total = None
    for g, win in enumerate(POOL_WINDOWS):
        acc = acc + pltpu.roll(acc, win // 2, axis=0)
        in_group = (lane >= g * 64) & (lane < (g + 1) * 64)
        total = jnp.where(in_group, acc, 0.0) if total is None else jnp.where(in_group, acc, total)
    win_sum = total[POOL_HALO:]
    row = lax.broadcasted_iota(jnp.int32, (t, POOL_WIDTH), 0) + first_row
    lane_t = lax.broadcasted_iota(jnp.int32, (t, POOL_WIDTH), 1)
    win_lane = jnp.left_shift(2, lane_t // 64)
    cnt = jnp.minimum(row + 1, win_lane).astype(jnp.float32)
    d = (win_sum / cnt - a).astype(jnp.bfloat16)
    y = jnp.dot(d, pool_w_bd, preferred_element_type=jnp.float32)
    return y * pool_scale


def _spatial_gating(u, v, ln_g, ln_b, w_masked, bias_full):
    t = u.shape[0]
    mu = jnp.mean(v, axis=-1, keepdims=True)
    var = jnp.mean(jnp.square(v - mu), axis=-1, keepdims=True)
    vn = ((v - mu) * lax.rsqrt(var + LN_EPS) * ln_g + ln_b).astype(jnp.bfloat16)
    lane = lax.broadcasted_iota(jnp.int32, (SGU_BLOCK, SGU_WIDTH), 1)
    outs = []
    for n in range(t // SGU_BLOCK):
        vb = vn[n * SGU_BLOCK:(n + 1) * SGU_BLOCK]
        mixed = bias_full
        for h in range(SGU_HEADS):
            r = jnp.dot(w_masked[h], vb, preferred_element_type=jnp.float32)
            in_head = (lane >= h * HEAD_DIM) & (lane < (h + 1) * HEAD_DIM)
            mixed = mixed + jnp.where(in_head, r, 0.0)
        outs.append(mixed)
    return u * jnp.concatenate(outs, axis=0)


def _attention(qt_ref, k_ref, vt_ref, carry_scr, acc_scr, first_group, groups):
    two = 2 * ATTN_BLOCK
    r = lax.broadcasted_iota(jnp.int32, (two, two), 0)
    c = lax.broadcasted_iota(jnp.int32, (two, two), 1)
    same_head = (r >= ATTN_BLOCK) == (c >= ATTN_BLOCK)
    tri2 = jnp.where(same_head & (c >= r), 1.0, 0.0).astype(jnp.bfloat16)
    blocks_per_group = ATTN_GROUP // ATTN_BLOCK
    key = lax.broadcasted_iota(jnp.int32, (two, ATTN_GROUP), 0) & (ATTN_BLOCK - 1)
    query = lax.broadcasted_iota(jnp.int32, (two, ATTN_GROUP), 1)
    causal = {t: key + (blocks_per_group - 1 - t) * ATTN_BLOCK < query for t in range(blocks_per_group)}

    def tile_keys(group, step):
        key_block = (first_group + group + 1) * blocks_per_group - 1 - step
        if isinstance(key_block, int):
            return key_block, key_block * two
        return key_block, pl.multiple_of(jnp.maximum(key_block, 0) * two, two)

    def logits(pair, group, step):
        cols = slice(group * ATTN_GROUP, (group + 1) * ATTN_GROUP)
        _, k0 = tile_keys(group, step)
        z = jnp.dot(k_ref[0, pair, pl.ds(k0, two), :], qt_ref[0, pair, :, cols],
                    preferred_element_type=jnp.float32)
        if isinstance(step, int) and step < blocks_per_group:
            z = jnp.where(causal[step], z, ATTN_MASKED)
        zb = z.astype(jnp.bfloat16)
        return z, jnp.maximum(zb, 0.0) + jnp.log(1.0 + jnp.exp(-jnp.abs(zb)))

    def weights(pair, group, step, z, spb):
        cols = slice(group * ATTN_GROUP, (group + 1) * ATTN_GROUP)
        key_block, _ = tile_keys(group, step)
        suffix = jnp.dot(tri2, spb, preferred_element_type=jnp.float32)
        ws, carries = [], []
        for h in range(2):
            rows = slice(h * ATTN_BLOCK, (h + 1) * ATTN_BLOCK)
            total = suffix[h * ATTN_BLOCK:h * ATTN_BLOCK + 1, :]
            x = z[rows] - suffix[rows]
            if isinstance(step, int) and step == 0:
                carry = total
            else:
                old = carry_scr[pair, h:h + 1, cols]
                if not isinstance(step, int):
                    old = old + jnp.where(key_block >= 0, 0.0, -ATTN_MASKED)
                x = x - old
                carry = old + total
            carry_scr[pair, h:h + 1, cols] = carry
            ws.append(jnp.exp(x).astype(jnp.bfloat16))
            carries.append(carry)
        return jnp.concatenate(ws, axis=0), jnp.minimum(carries[0], carries[1])

    def accumulate(pair, group, step, w):
        cols = slice(group * ATTN_GROUP, (group + 1) * ATTN_GROUP)
        _, k0 = tile_keys(group, step)
        out = jnp.dot(vt_ref[0, pair, :, pl.ds(k0, two)], w,
                      preferred_element_type=jnp.float32)
        if isinstance(step, int) and step == 0:
            acc_scr[pair, :, cols] = out
        else:
            acc_scr[pair, :, cols] += out

    tiles = [(pair, group) for group in range(groups) for pair in range(SB_PAIRS)]

    def visit(work):
        n = len(work)
        zs, ws, carries = {}, {}, {}
        lag_w, lag_a = ATTN_STAGE_LAGS
        for i in range(n + lag_a):
            if i < n:
                zs[i] = logits(*work[i])
            if 0 <= i - lag_w < n:
                ws[i - lag_w], carries[i - lag_w] = weights(*work[i - lag_w], *zs.pop(i - lag_w))
            if 0 <= i - lag_a < n:
                accumulate(*work[i - lag_a], ws.pop(i - lag_a))
        return [(work[i], carries[i]) for i in range(n - len(tiles), n)]

    def more_to_do(last):
        smallest = None
        for (_, group, step), carry in last:
            key_block, _ = tile_keys(group, step)
            m = jnp.where(key_block > 0, carry, ATTN_SKIP)
            smallest = m if smallest is None else jnp.minimum(smallest, m)
        return jnp.min(smallest) < ATTN_SKIP

    if isinstance(first_group, int):
        assert first_group == 0
        most = groups * blocks_per_group
        visit([(pair, group, step) for step in range(most) for pair, group in tiles
               if step < (group + 1) * blocks_per_group])
        return

    def body(state):
        step, _ = state
        steps = [step + i for i in range(ATTN_STEPS_PER_VISIT)]
        return step + ATTN_STEPS_PER_VISIT, more_to_do(
            visit([(pair, group, st) for st in steps for pair, group in tiles]))

    first_steps = min(ATTN_FIRST_VISIT_STEPS, (groups + 1) * blocks_per_group)
    live = more_to_do(visit([(pair, group, step) for step in range(first_steps) for pair, group in tiles]))
    lax.while_loop(lambda state: state[1], body, (jnp.int32(first_steps), live))


def _mixer_kernel(yab_ref, gc_ref, qt_ref, k_ref, vt_ref, x_ref, gate_ref,
                  wout_ref, lng_ref, lnb_ref, o_ref, y_scr, att_scr, carry_scr, *, alpha):
    s = pl.program_id(1)
    t = x_ref.shape[1]

    groups_per_tile = t // ATTN_GROUP

    @pl.when(s == 0)
    def _():
        _attention(qt_ref, k_ref, vt_ref, carry_scr, att_scr, 0, groups_per_tile)

    @pl.when(s > 0)
    def _():
        _attention(qt_ref, k_ref, vt_ref, carry_scr, att_scr, s * groups_per_tile, groups_per_tile)

    y_scr[:, 0:POOL_WIDTH + SGU_WIDTH] = yab_ref[0]
    for p in range(SB_PAIRS):
        cols = slice(p * 128, (p + 1) * 128)
        y_scr[:, SB_WIDTH + p * 128:SB_WIDTH + (p + 1) * 128] = (
            att_scr[p].T * _silu(gc_ref[0, :, cols])).astype(jnp.bfloat16)

    for i in range(t // OUT_ROWS):
        rows = slice(i * OUT_ROWS, (i + 1) * OUT_ROWS)
        y = jnp.dot(y_scr[rows, :], wout_ref[...], preferred_element_type=jnp.float32)
        r = alpha * x_ref[0, rows, :] + gate_ref[0] * y
        mu = jnp.mean(r, axis=-1, keepdims=True)
        var = jnp.mean(jnp.square(r - mu), axis=-1, keepdims=True)
        o_ref[0, rows, :] = (r - mu) * lax.rsqrt(var + LN_EPS) * lng_ref[...] + lnb_ref[...]


def _mixers(yab, gc, qt, k, vt, x, mod3, wout, ln_g, ln_b, alpha):
    batch, seq, d = x.shape
    t = SEQ_TILE
    full = lambda shape: pl.BlockSpec(shape, lambda b, s: (0,) * len(shape))
    return pl.pallas_call(
        functools.partial(_mixer_kernel, alpha=alpha),
        grid=(batch, seq // t),
        in_specs=[
            pl.BlockSpec((1, t, POOL_WIDTH + SGU_WIDTH), lambda b, s: (b, s, 0)),
            pl.BlockSpec((1, t, SB_WIDTH), lambda b, s: (b, s, 0)),
            pl.BlockSpec((1, SB_PAIRS, 128, t), lambda b, s: (b, 0, 0, s)),
            pl.BlockSpec((1, SB_PAIRS, 2 * seq, 128), lambda b, s: (b, 0, 0, 0)),
            pl.BlockSpec((1, SB_PAIRS, 128, 2 * seq), lambda b, s: (b, 0, 0, 0)),
            pl.BlockSpec((1, t, d), lambda b, s: (b, s, 0)),
            pl.BlockSpec((1, 1, d), lambda b, s: (b, 0, 2)),
            full(wout.shape), full(ln_g.shape), full(ln_b.shape),
        ],
        out_specs=pl.BlockSpec((1, t, d), lambda b, s: (b, s, 0)),
        out_shape=jax.ShapeDtypeStruct((batch, seq, d), jnp.float32),
        scratch_shapes=[
            pltpu.VMEM((t, d), jnp.bfloat16),
            pltpu.VMEM((SB_PAIRS, 128, t), jnp.float32),
            pltpu.VMEM((SB_PAIRS, 2, t), jnp.float32),
        ],
        compiler_params=pltpu.CompilerParams(
            dimension_semantics=("parallel", "arbitrary"),
            vmem_limit_bytes=VMEM_LIMIT_BYTES),
        name="mixers_outproj",
    )(yab, gc, qt, k, vt, x, mod3, wout, ln_g, ln_b)


def _block_diag(w):
    g, n, _ = w.shape
    out = jnp.zeros((g * n, g * n), w.dtype)
    for i in range(g):
        out = out.at[i * n:(i + 1) * n, i * n:(i + 1) * n].set(w[i])
    return out


def kernel(x, c, w_in, pool_w, pool_scale, sgu_ln_g, sgu_ln_b, sgu_w, sgu_b, w_out, ada_w, ada_b,
           ln_g, ln_b):
    depth = w_in.shape[0]
    batch = x.shape[0]
    alpha = (2 * depth) ** 0.25
    mod = _modulation(c, ada_w, ada_b)
    bf = jnp.bfloat16
    for l in range(depth):
        w = w_in[l]
        wf = jnp.concatenate([w[:, 0:1280], w[:, 2816:3328]], axis=1).astype(bf)
        wqt = w[:, 1280:1792].T.astype(bf)
        wk = w[:, 1792:2304].astype(bf)
        wvt = w[:, 2304:2816].T.astype(bf)
        mod3 = mod[l].reshape(batch, 1, 3 * D_MODEL)
        yab, gc, qt, k, vt = _in_projection(
            x, mod3, wf, wqt, wk, wvt,
            _block_diag(pool_w[l]).astype(bf), pool_scale[l].reshape(1, POOL_WIDTH),
            sgu_ln_g[l].reshape(1, SGU_WIDTH), sgu_ln_b[l].reshape(1, SGU_WIDTH),
            sgu_w[l], jnp.repeat(sgu_b[l].T, HEAD_DIM, axis=1))
        x = _mixers(yab, gc, qt, k, vt, x, mod3, w_out[l].astype(bf),
                    ln_g[l].reshape(1, D_MODEL), ln_b[l].reshape(1, D_MODEL), alpha)
    return x
```
